```python
import math
import jax, jax.numpy as jnp
from jax import lax
import numpy as np

D_MODEL = 1024
BATCH = 8
SEQ = 2048
DEPTH = 4
DEC_BATCH = 32
DEC_SEQ = 4
PAST_LEN = 8192
PAGE_SIZE = 128

N_MIXERS = 2
N_SB_LAYERS = (DEPTH + 1) // 2
N_RET_LAYERS = DEPTH // 2
SB_HEADS = 16
SB_HEAD_DIM = D_MODEL // SB_HEADS
SB_BLOCK = 128
SB_BIAS_NEAR = 1.0
SB_BIAS_FAR = 10.0
RET_HEADS = 4
RET_QK_DIM = D_MODEL // RET_HEADS
RET_V_DIM = 2 * D_MODEL // RET_HEADS
RET_CHUNK = 128
ROPE_BASE = 10000.0
N_EXPERTS = 32
TOP_K = 4
D_FF = D_MODEL
SWIGLU_LIMIT = 7.0
SWIGLU_ALPHA = 1.702
LN_EPS = 1e-5
DEEPNORM_ALPHA = (2 * DEPTH) ** 0.25
DEEPNORM_BETA = (8 * DEPTH) ** -0.25

kernel_name = 'sb_retention_moe_deepnorm_step'

F32 = jnp.float32


def layer_norm(x, g, b):
    xf = x.astype(F32)
    mu = jnp.mean(xf, axis=-1, keepdims=True)
    var = jnp.mean(jnp.square(xf - mu), axis=-1, keepdims=True)
    return ((xf - mu) * lax.rsqrt(var + LN_EPS) * g + b).astype(x.dtype)


def rope(a, pos):
    half = a.shape[-1] // 2
    inv = ROPE_BASE ** (-jnp.arange(half, dtype=F32) / half)
    ang = pos[:, None] * inv[None, :]
    cos = jnp.cos(ang)[None, :, None, :]
    sin = jnp.sin(ang)[None, :, None, :]
    a1, a2 = a[..., :half], a[..., half:]
    return jnp.concatenate([a1 * cos - a2 * sin, a1 * sin + a2 * cos], axis=-1)


def sb_attend(q, k, v, q_pos, k_pos, bias):
    z = (jnp.einsum('bqhd,bkhd->bhqk', q, k).astype(F32) * (SB_HEAD_DIM ** -0.5)
         + bias.astype(F32)[None, :, None, None])
    mask = (k_pos[None, :] < q_pos[:, None])[None, None]
    log_keep = jnp.where(mask, jax.nn.log_sigmoid(-z), 0.0)
    between = lax.cumsum(log_keep, axis=3, reverse=True) - log_keep
    a = jnp.where(mask, jnp.exp(jax.nn.log_sigmoid(z) + between), 0.0)
    return jnp.einsum('bhqk,bkhd->bqhd', a.astype(v.dtype), v)


def sb_mixer(x, w_qkv, w_out, bias, past_k, past_v):
    b, t, _ = x.shape
    qkv = (x @ w_qkv).reshape(b, t, 3, SB_HEADS, SB_HEAD_DIM)
    q, k, v = qkv[:, :, 0], qkv[:, :, 1], qkv[:, :, 2]
    if past_k is None:
        pos = jnp.arange(t)
        outs = []
        for i0 in range(0, t, SB_BLOCK):
            i1 = min(i0 + SB_BLOCK, t)
            outs.append(sb_attend(q[:, i0:i1], k[:, :i1], v[:, :i1], pos[i0:i1], pos[:i1], bias))
        o = jnp.concatenate(outs, axis=1)
    else:
        past = past_k.shape[1]
        k_all = jnp.concatenate([past_k, k], axis=1)
        v_all = jnp.concatenate([past_v, v], axis=1)
        o = sb_attend(q, k_all, v_all, past + jnp.arange(t), jnp.arange(past + t), bias)
    y = o.reshape(b, t, SB_HEADS * SB_HEAD_DIM) @ w_out
    return y, k, v


def retention_chunkwise(q, k, v, s0):
    b, t, h, _ = q.shape
    c = min(RET_CHUNK, t)
    n = t // c
    log_g = jnp.log1p(-jnp.exp2(-5.0 - jnp.arange(RET_HEADS, dtype=F32)))
    idx = jnp.arange(c, dtype=F32)
    rel = idx[:, None] - idx[None, :]
    intra = jnp.where(rel >= 0, jnp.exp(log_g[:, None, None] * jnp.maximum(rel, 0.0)), 0.0)
    q_dec = jnp.exp(log_g[None, :] * (idx[:, None] + 1.0))
    k_dec = jnp.exp(log_g[None, :] * (c - 1.0 - idx[:, None]))
    c_dec = jnp.exp(log_g * c)

    def to_chunks(a):
        return jnp.moveaxis(a.reshape(b, n, c, h, a.shape[-1]), 1, 0)

    def step(s, inp):
        qc, kc, vc = inp
        sc = jnp.einsum('bchd,bshd->bhcs', qc, kc) * intra
        o = (jnp.einsum('bhcs,bshv->bchv', sc, vc)
             + jnp.einsum('bchd,bhdv->bchv', qc, s) * q_dec[None, :, :, None])
        s = (c_dec[None, :, None, None] * s
             + jnp.einsum('bshd,bshv->bhdv', kc * k_dec[None, :, :, None], vc))
        return s, o

    s_fin, o = lax.scan(step, s0, (to_chunks(q), to_chunks(k), to_chunks(v)))
    return jnp.moveaxis(o, 0, 1).reshape(b, t, h, -1), s_fin


def retention_mixer(x, w_in, w_out, s0, pos0):
    b, t, _ = x.shape
    qk_w = RET_HEADS * RET_QK_DIM
    v_w = RET_HEADS * RET_V_DIM
    p = (x @ w_in).astype(F32)
    q = p[..., :qk_w].reshape(b, t, RET_HEADS, RET_QK_DIM)
    k = p[..., qk_w:2 * qk_w].reshape(b, t, RET_HEADS, RET_QK_DIM)
    v = p[..., 2 * qk_w:2 * qk_w + v_w].reshape(b, t, RET_HEADS, RET_V_DIM)
    g = p[..., 2 * qk_w + v_w:]
    pos = pos0 + jnp.arange(t, dtype=F32)
    q = rope(q, pos)
    k = rope(k, pos) * (RET_QK_DIM ** -0.5)
    o, s = retention_chunkwise(q, k, v, s0.astype(F32))
    mu = jnp.mean(o, axis=-1, keepdims=True)
    var = jnp.mean(jnp.square(o - mu), axis=-1, keepdims=True)
    o = (o - mu) * lax.rsqrt(var + LN_EPS)
    y = (jax.nn.silu(g) * o.reshape(b, t, v_w)).astype(x.dtype) @ w_out
    return y, s


def _moe_rows_per_block(n_assign):
    r = n_assign // N_EXPERTS
    return int(max(8, min(256, 8 * (r // 8))))


def moe_ffn(x, w_router, b_router, w_up, b_up, w_down, b_down):
    b, t, d = x.shape
    xt = x.reshape(b * t, d)
    n_tok = b * t
    logits = (xt @ w_router).astype(F32) + b_router.astype(F32)
    top_val, top_idx = lax.top_k(logits, TOP_K)
    gates = jax.nn.softmax(top_val, axis=-1)
    n_as = n_tok * TOP_K
    flat_e = top_idx.reshape(-1)
    order = jnp.argsort(flat_e)
    sorted_e = flat_e[order]
    sorted_tok = order // TOP_K
    sorted_gate = gates.reshape(-1)[order]
    blk = _moe_rows_per_block(n_as)
    n_blocks = -(-(n_as + N_EXPERTS * (blk - 1)) // blk)
    counts = jnp.bincount(flat_e, length=N_EXPERTS)
    padded = (counts + blk - 1) // blk * blk
    pad_end = jnp.cumsum(padded)
    pad_start = pad_end - padded
    start = jnp.cumsum(counts) - counts
    dest = pad_start[sorted_e] + jnp.arange(n_as) - start[sorted_e]
    rows = jnp.full((n_blocks * blk,), n_tok, jnp.int32).at[dest].set(sorted_tok.astype(jnp.int32))
    x_pad = jnp.concatenate([xt, jnp.zeros((1, d), xt.dtype)], axis=0)
    xs = x_pad[rows].reshape(n_blocks, blk, d)
    block_e = jnp.minimum(jnp.searchsorted(pad_end, jnp.arange(n_blocks) * blk, side='right'), N_EXPERTS - 1)

    def expert(args):
        xb, e = args
        h = xb @ w_up[e] + b_up[e]
        gate = jnp.minimum(h[:, ::2], SWIGLU_LIMIT)
        up = jnp.clip(h[:, 1::2], -SWIGLU_LIMIT, SWIGLU_LIMIT)
        glu = gate * jax.nn.sigmoid(gate * SWIGLU_ALPHA)
        return ((up + 1.0) * glu) @ w_down[e] + b_down[e]

    ys = lax.map(expert, (xs, block_e)).reshape(n_blocks * blk, d)
    contrib = ys[dest].astype(F32) * sorted_gate[:, None]
    y = jax.ops.segment_sum(contrib, sorted_tok, num_segments=n_tok)
    return y.astype(x.dtype).reshape(b, t, d)


def run_trunk(x, pos0, cache_k, cache_v, page_table, state_ret,
              w_sb_qkv, w_sb_out, sb_bias, w_ret_in, w_ret_out, ln_mix_g, ln_mix_b, ln_ffn_g, ln_ffn_b,
              w_router, b_router, w_exp_up, b_exp_up, w_exp_down, b_exp_down):
    b = x.shape[0]
    new_k, new_v, new_s = [], [], []
    for i in range(DEPTH):
        j = i // N_MIXERS
        if i % N_MIXERS == 0:
            if cache_k is None:
                pk = pv = None
            else:
                pk = cache_k[j][page_table].reshape(b, -1, SB_HEADS, SB_HEAD_DIM)
                pv = cache_v[j][page_table].reshape(b, -1, SB_HEADS, SB_HEAD_DIM)
            h, k, v = sb_mixer(x, w_sb_qkv[j], w_sb_out[j], sb_bias[j], pk, pv)
            new_k.append(k)
            new_v.append(v)
        else:
            if state_ret is None:
                s0 = jnp.zeros((b, RET_HEADS, RET_QK_DIM, RET_V_DIM), F32)
            else:
                s0 = state_ret[j]
            h, s = retention_mixer(x, w_ret_in[j], w_ret_out[j], s0, pos0)
            new_s.append(s)
        x = layer_norm(DEEPNORM_ALPHA * x + h, ln_mix_g[i], ln_mix_b[i])
        f = moe_ffn(x, w_router[i], b_router[i], w_exp_up[i], b_exp_up[i], w_exp_down[i], b_exp_down[i])
        x = layer_norm(DEEPNORM_ALPHA * x + f, ln_ffn_g[i], ln_ffn_b[i])
    return x, jnp.stack(new_k), jnp.stack(new_v), jnp.stack(new_s)


def setup_inputs(seed: int = 0) -> dict:
    key = jax.random.key(seed)
    ks = jax.random.split(key, 21)
    nrm = jax.random.normal
    n_pages = PAST_LEN // PAGE_SIZE
    n_used = DEC_BATCH * n_pages
    n_pool = n_used + n_used // 4
    sb_w = SB_HEADS * SB_HEAD_DIM
    qk_w = RET_HEADS * RET_QK_DIM
    v_w = RET_HEADS * RET_V_DIM
    x_prompt = nrm(ks[0], (BATCH, SEQ, D_MODEL), F32)
    x_sample = nrm(ks[1], (DEC_BATCH, DEC_SEQ, D_MODEL), F32)
    cache_k = nrm(ks[2], (N_SB_LAYERS, n_pool, PAGE_SIZE, SB_HEADS, SB_HEAD_DIM), F32)
    cache_v = nrm(ks[3], (N_SB_LAYERS, n_pool, PAGE_SIZE, SB_HEADS, SB_HEAD_DIM), F32)
    state_ret = 0.3 * nrm(ks[4], (N_RET_LAYERS, DEC_BATCH, RET_HEADS, RET_QK_DIM, RET_V_DIM), F32)
    page_table = jax.random.permutation(ks[5], n_pool)[:n_used].reshape(DEC_BATCH, n_pages).astype(jnp.int32)
    sb_col = jnp.concatenate([jnp.ones((2 * sb_w,), F32), jnp.full((sb_w,), DEEPNORM_BETA, F32)])
    w_sb_qkv = nrm(ks[6], (N_SB_LAYERS, D_MODEL, 3 * sb_w), F32) * (D_MODEL ** -0.5) * sb_col
    w_sb_out = nrm(ks[7], (N_SB_LAYERS, sb_w, D_MODEL), F32) * (sb_w ** -0.5) * DEEPNORM_BETA
    sb_bias = (-jnp.linspace(SB_BIAS_NEAR, SB_BIAS_FAR, SB_HEADS, dtype=F32)[None, :]
               + 0.01 * nrm(ks[20], (N_SB_LAYERS, SB_HEADS), F32))
    ret_col = jnp.concatenate([jnp.ones((2 * qk_w,), F32), jnp.full((v_w,), DEEPNORM_BETA, F32),
                               jnp.ones((v_w,), F32)])
    w_ret_in = nrm(ks[8], (N_RET_LAYERS, D_MODEL, 2 * qk_w + 2 * v_w), F32) * (D_MODEL ** -0.5) * ret_col
    w_ret_out = nrm(ks[9], (N_RET_LAYERS, v_w, D_MODEL), F32) * (v_w ** -0.5) * DEEPNORM_BETA
    ln_mix_g = 1.0 + 0.02 * nrm(ks[10], (DEPTH, D_MODEL), F32)
    ln_mix_b = 0.02 * nrm(ks[11], (DEPTH, D_MODEL), F32)
    ln_ffn_g = 1.0 + 0.02 * nrm(ks[12], (DEPTH, D_MODEL), F32)
    ln_ffn_b = 0.02 * nrm(ks[13], (DEPTH, D_MODEL), F32)
    w_router = nrm(ks[14], (DEPTH, D_MODEL, N_EXPERTS), F32) * (D_MODEL ** -0.5)
    b_router = 0.01 * nrm(ks[15], (DEPTH, N_EXPERTS), F32)
    w_exp_up = nrm(ks[16], (DEPTH, N_EXPERTS, D_MODEL, 2 * D_FF), F32) * (D_MODEL ** -0.5) * DEEPNORM_BETA
    b_exp_up = 0.02 * nrm(ks[17], (DEPTH, N_EXPERTS, 2 * D_FF), F32)
    w_exp_down = nrm(ks[18], (DEPTH, N_EXPERTS, D_FF, D_MODEL), F32) * (D_FF ** -0.5) * DEEPNORM_BETA
    b_exp_down = 0.02 * nrm(ks[19], (DEPTH, N_EXPERTS, D_MODEL), F32)
    return {'x_prompt': x_prompt, 'x_sample': x_sample, 'cache_k': cache_k, 'cache_v': cache_v,
            'state_ret': state_ret, 'page_table': page_table,
            'w_sb_qkv': w_sb_qkv, 'w_sb_out': w_sb_out, 'sb_bias': sb_bias,
            'w_ret_in': w_ret_in, 'w_ret_out': w_ret_out,
            'ln_mix_g': ln_mix_g, 'ln_mix_b': ln_mix_b, 'ln_ffn_g': ln_ffn_g, 'ln_ffn_b': ln_ffn_b,
            'w_router': w_router, 'b_router': b_router, 'w_exp_up': w_exp_up, 'b_exp_up': b_exp_up,
            'w_exp_down': w_exp_down, 'b_exp_down': b_exp_down}


def reference(x_prompt, x_sample, cache_k, cache_v, state_ret, page_table,
              w_sb_qkv, w_sb_out, sb_bias, w_ret_in, w_ret_out, ln_mix_g, ln_mix_b, ln_ffn_g, ln_ffn_b,
              w_router, b_router, w_exp_up, b_exp_up, w_exp_down, b_exp_down):
    y_prompt, k_prompt, v_prompt, state_prompt = run_trunk(
        x_prompt, 0, None, None, None, None,
        w_sb_qkv, w_sb_out, sb_bias, w_ret_in, w_ret_out, ln_mix_g, ln_mix_b, ln_ffn_g, ln_ffn_b,
        w_router, b_router, w_exp_up, b_exp_up, w_exp_down, b_exp_down)
    y_sample, k_sample, v_sample, state_sample = run_trunk(
        x_sample, PAST_LEN, cache_k, cache_v, page_table, state_ret,
        w_sb_qkv, w_sb_out, sb_bias, w_ret_in, w_ret_out, ln_mix_g, ln_mix_b, ln_ffn_g, ln_ffn_b,
        w_router, b_router, w_exp_up, b_exp_up, w_exp_down, b_exp_down)
    return (y_prompt, y_sample, k_prompt, v_prompt, state_prompt, k_sample, v_sample, state_sample)
```

```python
import functools
import math

import jax
import jax.numpy as jnp
from jax import lax
from jax.experimental import pallas as pl
from jax.experimental.pallas import tpu as pltpu

F32 = jnp.float32
BF16 = jnp.bfloat16

TOP_K = 4
ROPE_BASE = 10000.0
LN_EPS = 1e-5
SWIGLU_LIMIT = 7.0
SWIGLU_ALPHA = 1.702

LANES = 128
SUBLANES = 8
MXU_DIM = 256
SAMPLE_PAD = SUBLANES
VMEM_LIMIT_BYTES = 56 * 1024 * 1024
ROW_TILE_TARGET = 1024
MOE_BLOCK_ROWS = 256
SB_BLOCK = 128
RET_CHUNK = 256
RET_DECODE_ROWS = 128
ROUTER_TILE_TARGET = 640
DECODE_PAGES_PER_STEP = 4


def _params(*sem):
    return pltpu.CompilerParams(dimension_semantics=sem, vmem_limit_bytes=VMEM_LIMIT_BYTES)


def _tile(n, target, unit):
    best = None
    for t in range(unit, min(n, target) + 1, unit):
        if n % t == 0:
            best = t
    assert best is not None, (n, target, unit)
    return best


def _layer_norm(y, g, b):
    mu = jnp.mean(y, axis=-1, keepdims=True)
    yc = y - mu
    var = jnp.mean(yc * yc, axis=-1, keepdims=True)
    return yc * lax.rsqrt(var + LN_EPS) * g + b


def _keep_rows(tile_idx, tm, n_prompt, dec_seq):
    row = tile_idx * tm + lax.broadcasted_iota(jnp.int32, (tm, 1), 0)
    pad = (row >= n_prompt) & (((row - n_prompt) & (SAMPLE_PAD - 1)) >= dec_seq)
    return jnp.logical_not(pad)


def _stacked_proj_kernel(x_ref, w_ref, o_ref, *, first_scale):
    y = jnp.dot(x_ref[...].astype(BF16), w_ref[...], preferred_element_type=F32)
    if first_scale != 1.0:
        y = y * jnp.where(pl.program_id(1) == 0, first_scale, 1.0)
    o_ref[...] = y


def stacked_proj(x, w_bf16, n_groups, first_scale=1.0):
    n, k = x.shape
    width = w_bf16.shape[1] // n_groups
    tm = _tile(n, ROW_TILE_TARGET, SUBLANES)
    return pl.pallas_call(
        functools.partial(_stacked_proj_kernel, first_scale=first_scale),
        grid=(n // tm, n_groups),
        in_specs=[pl.BlockSpec((tm, k), lambda i, j: (i, 0)),
                  pl.BlockSpec((k, width), lambda i, j: (0, j))],
        out_specs=pl.BlockSpec((None, tm, width), lambda i, j: (j, i, 0)),
        out_shape=jax.ShapeDtypeStruct((n_groups, n, width), F32),
        compiler_params=_params("parallel", "arbitrary"),
        name="stacked_proj",
    )(x, w_bf16)


def _out_proj_ln_kernel(a_ref, w_ref, x_ref, g_ref, b_ref, o_ref, ob_ref, *, alpha, n_prompt, dec_seq):
    h = jnp.dot(a_ref[...], w_ref[...], preferred_element_type=F32)
    y = _layer_norm(alpha * x_ref[...] + h, g_ref[...], b_ref[...])
    keep = _keep_rows(pl.program_id(0), o_ref.shape[0], n_prompt, dec_seq)
    y = jnp.where(keep, y, 0.0)
    o_ref[...] = y
    ob_ref[...] = y.astype(BF16)


def out_proj_ln(a, w_bf16, x, g, b, alpha, n_prompt, dec_seq):
    n, k = a.shape
    d = w_bf16.shape[1]
    tm = _tile(n, ROW_TILE_TARGET, SUBLANES)
    return pl.pallas_call(
        functools.partial(_out_proj_ln_kernel, alpha=alpha, n_prompt=n_prompt, dec_seq=dec_seq),
        grid=(n // tm,),
        in_specs=[pl.BlockSpec((tm, k), lambda i: (i, 0)),
                  pl.BlockSpec((k, d), lambda i: (0, 0)),
                  pl.BlockSpec((tm, d), lambda i: (i, 0)),
                  pl.BlockSpec((1, d), lambda i: (0, 0)),
                  pl.BlockSpec((1, d), lambda i: (0, 0))],
        out_specs=[pl.BlockSpec((tm, d), lambda i: (i, 0))] * 2,
        out_shape=[jax.ShapeDtypeStruct((n, d), F32), jax.ShapeDtypeStruct((n, d), BF16)],
        compiler_params=_params("parallel"),
        name="out_proj_ln",
    )(a, w_bf16, x, g.reshape(1, d), b.reshape(1, d))


def _residual_ln_kernel(x_ref, f_ref, g_ref, b_ref, o_ref, *, alpha, n_prompt, dec_seq):
    y = _layer_norm(alpha * x_ref[...] + f_ref[...], g_ref[...], b_ref[...])
    keep = _keep_rows(pl.program_id(0), o_ref.shape[0], n_prompt, dec_seq)
    o_ref[...] = jnp.where(keep, y, 0.0)


def residual_ln(x, f, g, b, alpha, n_prompt, dec_seq):
    n, d = x.shape
    tm = _tile(n, ROW_TILE_TARGET, SUBLANES)
    row = pl.BlockSpec((tm, d), lambda i: (i, 0))
    vec = pl.BlockSpec((1, d), lambda i: (0, 0))
    return pl.pallas_call(
        functools.partial(_residual_ln_kernel, alpha=alpha, n_prompt=n_prompt, dec_seq=dec_seq),
        grid=(n // tm,),
        in_specs=[row, row, vec, vec],
        out_specs=row,
        out_shape=jax.ShapeDtypeStruct((n, d), F32),
        compiler_params=_params("parallel"),
        name="residual_ln",
    )(x, f, g.reshape(1, d), b.reshape(1, d))


def _softplus(z):
    return jnp.maximum(z, 0.0) + jnp.log1p(jnp.exp(-jnp.abs(z)))


def _suffix_sum(sp, tri):
    hi = sp.astype(BF16)
    lo = (sp - hi.astype(F32)).astype(BF16)
    return (jnp.dot(hi, tri, preferred_element_type=F32)
            + jnp.dot(lo, tri, preferred_element_type=F32))


def _tri_incl(n):
    r = lax.broadcasted_iota(jnp.int32, (n, n), 0)
    c = lax.broadcasted_iota(jnp.int32, (n, n), 1)
    return jnp.where(r >= c, 1.0, 0.0).astype(BF16)


def _sb_tile(q2, kb, vb, bias_col, carry, tri, mask):
    z = lax.dot_general(q2, kb, (((1,), (1,)), ((), ())), preferred_element_type=F32) + bias_col
    sp = _softplus(z)
    if mask is not None:
        sp = jnp.where(mask, sp, 0.0)
    incl = _suffix_sum(sp, tri)
    a = jnp.exp(z - incl - carry)
    if mask is not None:
        a = jnp.where(mask, a, 0.0)
    pv = jnp.dot(a.astype(BF16), vb, preferred_element_type=F32)
    return pv, carry + incl[:, 0:1]


def _sb_prompt_kernel(bias_ref, q_ref, k_ref, v_ref, o_in_ref, o_ref, *, head_dim, blk):
    del o_in_ref
    pair = pl.program_id(1)
    n_blk = q_ref.shape[0] // blk
    first = lax.broadcasted_iota(jnp.int32, (blk, LANES), 1) < head_dim
    rows2 = lax.broadcasted_iota(jnp.int32, (2 * blk, 1), 0)
    bias_col = jnp.where(rows2 < blk, bias_ref[2 * pair], bias_ref[2 * pair + 1])
    tri = _tri_incl(blk)
    r2 = lax.broadcasted_iota(jnp.int32, (2 * blk, blk), 0)
    c2 = lax.broadcasted_iota(jnp.int32, (2 * blk, blk), 1)
    diag2 = c2 < jnp.where(r2 < blk, r2, r2 - blk)

    def q_block(i, _):
        q0 = pl.multiple_of(i * blk, blk)
        q = q_ref[pl.ds(q0, blk), :]
        q2 = jnp.concatenate([jnp.where(first, q, 0.0), jnp.where(first, 0.0, q)], axis=0).astype(BF16)
        kb = k_ref[pl.ds(q0, blk), :].astype(BF16)
        vb = v_ref[pl.ds(q0, blk), :].astype(BF16)
        acc, carry = _sb_tile(q2, kb, vb, bias_col, jnp.zeros((2 * blk, 1), F32), tri, diag2)

        def k_block(jj, state):
            acc, carry = state
            k0 = pl.multiple_of((i - 1 - jj) * blk, blk)
            kb = k_ref[pl.ds(k0, blk), :].astype(BF16)
            vb = v_ref[pl.ds(k0, blk), :].astype(BF16)
            pv, carry = _sb_tile(q2, kb, vb, bias_col, carry, tri, None)
            return acc + pv, carry

        acc, _ = lax.fori_loop(0, i, k_block, (acc, carry))
        o_ref[pl.ds(q0, blk), :] = jnp.where(first, acc[:blk], acc[blk:]).astype(o_ref.dtype)
        return 0

    lax.fori_loop(0, n_blk, q_block, 0)


def sb_prompt_attention(qkv, bias, o_init, n_seq, t_len, head_dim):
    _, n, d = qkv.shape
    assert 2 * head_dim == LANES
    part = lambda g: pl.BlockSpec((None, t_len, LANES), lambda b, p: (g, b, p))
    return pl.pallas_call(
        functools.partial(_sb_prompt_kernel, head_dim=head_dim, blk=SB_BLOCK),
        grid=(n_seq, d // LANES),
        in_specs=[pl.BlockSpec(memory_space=pltpu.SMEM), part(0), part(1), part(2),
                  pl.BlockSpec(memory_space=pl.ANY)],
        out_specs=pl.BlockSpec((t_len, LANES), lambda b, p: (b, p)),
        out_shape=jax.ShapeDtypeStruct((n, d), BF16),
        input_output_aliases={4: 0},
        compiler_params=_params("parallel", "parallel"),
        name="sb_prompt_attention",
    )(bias, qkv, qkv, qkv, o_init)


def _sb_decode_kernel(pt_ref, bias_ref, q_ref, kn_ref, vn_ref, *rest,
                      n_heads, head_dim, pages_per_step, dec_seq):
    del pt_ref
    g = pages_per_step
    k_refs, v_refs = rest[:g], rest[g:2 * g]
    o_in_ref, o_ref, acc_sc, carry_sc, qbd_sc, kpad_sc, vpad_sc = rest[2 * g:]
    del o_in_ref
    step = pl.program_id(1)
    n_rows, d = qbd_sc.shape
    page = kpad_sc.shape[0]
    tri = _tri_incl(page)
    lane_head = lax.broadcasted_iota(jnp.int32, (n_rows, d), 1) // head_dim
    row_head = lax.broadcasted_iota(jnp.int32, (n_rows, d), 0) & (n_heads - 1)
    own = lane_head == row_head
    bias_col = bias_ref[...]

    def page_update(kb, vb, mask):
        pv, carry = _sb_tile(qbd_sc[...], kb, vb, bias_col, carry_sc[...], tri, mask)
        acc_sc[...] += pv
        carry_sc[...] = carry

    @pl.when(step == 0)
    def _():
        q = q_ref[...]
        qb = jnp.broadcast_to(q[:, None, :], (SAMPLE_PAD, n_heads, d)).reshape(n_rows, d)
        qbd_sc[...] = jnp.where(own, qb, 0.0).astype(BF16)
        acc_sc[...] = jnp.zeros_like(acc_sc)
        carry_sc[...] = jnp.zeros_like(carry_sc)
        kpad_sc[...] = jnp.zeros_like(kpad_sc)
        vpad_sc[...] = jnp.zeros_like(vpad_sc)
        kpad_sc[0:SAMPLE_PAD, :] = kn_ref[...]
        vpad_sc[0:SAMPLE_PAD, :] = vn_ref[...]
        t_row = lax.broadcasted_iota(jnp.int32, (n_rows, page), 0) // n_heads
        s_col = lax.broadcasted_iota(jnp.int32, (n_rows, page), 1)
        page_update(kpad_sc[...].astype(BF16), vpad_sc[...].astype(BF16),
                    (s_col < t_row) & (s_col < dec_seq))

    for kr, vr in zip(k_refs, v_refs):
        page_update(kr[...].astype(BF16), vr[...].astype(BF16), None)

    @pl.when(step == pl.num_programs(1) - 1)
    def _():
        acc = jnp.where(own, acc_sc[...], 0.0).reshape(SAMPLE_PAD, n_heads, d)
        o_ref[...] = jnp.sum(acc, axis=1).astype(o_ref.dtype)


def sb_decode_attention(qkv, bias, cache_k, cache_v, page_table, layer, o_init,
                        n_prompt, n_heads, head_dim, dec_seq):
    _, n, d = qkv.shape
    n_layers, n_pool, page = cache_k.shape[:3]
    n_seq, n_pages = page_table.shape
    g = DECODE_PAGES_PER_STEP
    assert n_pages % g == 0 and n_prompt % SAMPLE_PAD == 0
    assert n_heads & (n_heads - 1) == 0
    ck = cache_k.reshape(n_layers, n_pool, page, d)
    cv = cache_v.reshape(n_layers, n_pool, page, d)
    n_rows = SAMPLE_PAD * n_heads
    base = n_prompt // SAMPLE_PAD
    tok = pl.BlockSpec((SAMPLE_PAD, d), lambda b, s, pt: (base + b, 0))
    part = lambda gi: pl.BlockSpec((None, SAMPLE_PAD, d), lambda b, s, pt: (gi, base + b, 0))

    def page_spec(j):
        return pl.BlockSpec(
            (None, None, page, d),
            lambda b, s, pt: (layer, pt[b * n_pages + (n_pages - 1 - (s * g + j))], 0, 0))

    kv_specs = [page_spec(j) for j in range(g)]
    bias_rows = jnp.tile(bias, SAMPLE_PAD).reshape(n_rows, 1)
    return pl.pallas_call(
        functools.partial(_sb_decode_kernel, n_heads=n_heads, head_dim=head_dim,
                          pages_per_step=g, dec_seq=dec_seq),
        grid_spec=pltpu.PrefetchScalarGridSpec(
            num_scalar_prefetch=1,
            grid=(n_seq, n_pages // g),
            in_specs=[pl.BlockSpec((n_rows, 1), lambda b, s, pt: (0, 0)), part(0), part(1), part(2)]
                     + kv_specs + kv_specs + [pl.BlockSpec(memory_space=pl.ANY)],
            out_specs=tok,
            scratch_shapes=[pltpu.VMEM((n_rows, d), F32), pltpu.VMEM((n_rows, 1), F32),
                            pltpu.VMEM((n_rows, d), BF16), pltpu.VMEM((page, d), F32),
                            pltpu.VMEM((page, d), F32)]),
        out_shape=jax.ShapeDtypeStruct((n, d), BF16),
        input_output_aliases={5 + 2 * g: 0},
        compiler_params=_params("parallel", "arbitrary"),
        name="sb_decode_attention",
    )(page_table.reshape(-1), bias_rows, qkv, qkv, qkv, *([ck] * g), *([cv] * g), o_init)


def _rope(a, cos, sin):
    half = a.shape[-1] // 2
    a1, a2 = a[:, :half], a[:, half:]
    return jnp.concatenate([a1 * cos - a2 * sin, a1 * sin + a2 * cos], axis=-1)


def _retention_chunk(p_ref, cos, sin, s_sc, store, *, n_heads, dk, dv, valid):
    c_len = p_ref.shape[1]
    qk_w = n_heads * dk
    idx = lax.broadcasted_iota(jnp.int32, (c_len, 1), 0).astype(F32)
    rel = (lax.broadcasted_iota(jnp.int32, (c_len, c_len), 0)
           - lax.broadcasted_iota(jnp.int32, (c_len, c_len), 1)).astype(F32)
    live = lax.broadcasted_iota(jnp.int32, (c_len, 1), 0) < valid
    nt = (((1,), (1,)), ((), ()))
    tn = (((0,), (0,)), ((), ()))
    for h in range(n_heads):
        log_g = math.log1p(-2.0 ** (-5.0 - h))
        q = _rope(p_ref[0, :, h * dk:(h + 1) * dk], cos, sin)
        k = _rope(p_ref[0, :, qk_w + h * dk:qk_w + (h + 1) * dk], cos, sin) * (dk ** -0.5)
        k = jnp.where(live, k, 0.0)
        v = p_ref[1, :, h * dv:(h + 1) * dv].astype(BF16)
        gate = p_ref[2, :, h * dv:(h + 1) * dv]
        intra = jnp.where(rel >= 0.0, jnp.exp(log_g * jnp.maximum(rel, 0.0)), 0.0)
        q_dec = jnp.exp(log_g * (idx + 1.0))
        k_dec = jnp.exp(log_g * (valid - 1.0 - idx))
        qb = q.astype(BF16)
        sc = lax.dot_general(qb, k.astype(BF16), nt, preferred_element_type=F32) * intra
        s_old = s_sc[h]
        o = (jnp.dot(sc.astype(BF16), v, preferred_element_type=F32)
             + jnp.dot(qb, s_old.astype(BF16), preferred_element_type=F32) * q_dec)
        s_sc[h] = (math.exp(log_g * valid) * s_old
                   + lax.dot_general((k * k_dec).astype(BF16), v, tn, preferred_element_type=F32))
        mu = jnp.mean(o, axis=-1, keepdims=True)
        oc = o - mu
        var = jnp.mean(oc * oc, axis=-1, keepdims=True)
        on = oc * lax.rsqrt(var + LN_EPS)
        store(h, gate * (1.0 / (1.0 + jnp.exp(-gate))) * on)


def _ret_prompt_kernel(p_ref, cos_ref, sin_ref, y_in_ref, y_ref, s_out_ref, s_sc, *, n_heads, dk, dv):
    del y_in_ref
    c = pl.program_id(1)

    @pl.when(c == 0)
    def _():
        s_sc[...] = jnp.zeros_like(s_sc)

    def store(h, val):
        y_ref[:, h * dv:(h + 1) * dv] = val.astype(y_ref.dtype)

    _retention_chunk(p_ref, cos_ref[...], sin_ref[...], s_sc, store,
                     n_heads=n_heads, dk=dk, dv=dv, valid=p_ref.shape[1])

    @pl.when(c == pl.num_programs(1) - 1)
    def _():
        s_out_ref[...] = s_sc[...]


def retention_prompt(p, cos, sin, y_init, n_seq, t_len, n_heads, dk, dv):
    _, n, v_w = p.shape
    assert v_w == n_heads * dv == 2 * n_heads * dk
    chunk = _tile(t_len, RET_CHUNK, SUBLANES)
    n_chunks = t_len // chunk
    tab = pl.BlockSpec((chunk, dk // 2), lambda b, c: (c, 0))
    return pl.pallas_call(
        functools.partial(_ret_prompt_kernel, n_heads=n_heads, dk=dk, dv=dv),
        grid=(n_seq, n_chunks),
        in_specs=[pl.BlockSpec((3, chunk, v_w), lambda b, c: (0, b * n_chunks + c, 0)),
                  tab, tab, pl.BlockSpec(memory_space=pl.ANY)],
        out_specs=[pl.BlockSpec((chunk, v_w), lambda b, c: (b * n_chunks + c, 0)),
                   pl.BlockSpec((None, n_heads, dk, dv), lambda b, c: (b, 0, 0, 0))],
        out_shape=[jax.ShapeDtypeStruct((n, v_w), BF16),
                   jax.ShapeDtypeStruct((n_seq, n_heads, dk, dv), F32)],
        scratch_shapes=[pltpu.VMEM((n_heads, dk, dv), F32)],
        input_output_aliases={3: 0},
        compiler_params=_params("parallel", "arbitrary"),
        name="retention_prompt",
    )(p, cos, sin, y_init)


def _ret_decode_kernel(p_ref, cos_ref, sin_ref, s_in_ref, y_in_ref, y_ref, s_out_ref, p_sc, s_sc,
                       *, n_heads, dk, dv, dec_seq):
    del y_in_ref
    p_sc[...] = jnp.zeros_like(p_sc)
    p_sc[:, 0:SAMPLE_PAD, :] = p_ref[...]
    s_sc[...] = s_in_ref[...]

    def store(h, val):
        y_ref[:, h * dv:(h + 1) * dv] = val[0:SAMPLE_PAD].astype(y_ref.dtype)

    _retention_chunk(p_sc, cos_ref[...], sin_ref[...], s_sc, store,
                     n_heads=n_heads, dk=dk, dv=dv, valid=dec_seq)
    s_out_ref[...] = s_sc[...]


def retention_decode(p, cos, sin, state, layer, y_init, n_prompt, n_heads, dk, dv, dec_seq):
    _, n, v_w = p.shape
    n_seq = state.shape[1]
    base = n_prompt // SAMPLE_PAD
    tab = pl.BlockSpec((RET_DECODE_ROWS, dk // 2), lambda b: (0, 0))
    return pl.pallas_call(
        functools.partial(_ret_decode_kernel, n_heads=n_heads, dk=dk, dv=dv, dec_seq=dec_seq),
        grid=(n_seq,),
        in_specs=[pl.BlockSpec((3, SAMPLE_PAD, v_w), lambda b: (0, base + b, 0)), tab, tab,
                  pl.BlockSpec((None, None, n_heads, dk, dv), lambda b: (layer, b, 0, 0, 0)),
                  pl.BlockSpec(memory_space=pl.ANY)],
        out_specs=[pl.BlockSpec((SAMPLE_PAD, v_w), lambda b: (base + b, 0)),
                   pl.BlockSpec((None, n_heads, dk, dv), lambda b: (b, 0, 0, 0))],
        out_shape=[jax.ShapeDtypeStruct((n, v_w), BF16),
                   jax.ShapeDtypeStruct((n_seq, n_heads, dk, dv), F32)],
        scratch_shapes=[pltpu.VMEM((3, RET_DECODE_ROWS, v_w), F32),
                        pltpu.VMEM((n_heads, dk, dv), F32)],
        input_output_aliases={4: 0},
        compiler_params=_params("parallel"),
        name="retention_decode",
    )(p, cos, sin, state, y_init)


def _split_bf16(a):
    hi = a.astype(BF16)
    return hi, (a - hi.astype(F32)).astype(BF16)


def _router_kernel(x_ref, wt_ref, b_ref, idx_ref, gate_ref, rank_ref, cnt_ref, base_sc):
    n_exp = wt_ref.shape[0]
    tr = x_ref.shape[0]

    @pl.when(pl.program_id(0) == 0)
    def _():
        base_sc[...] = jnp.zeros_like(base_sc)

    nt = (((1,), (1,)), ((), ()))
    xh, xl = _split_bf16(x_ref[...])
    wh, wl = _split_bf16(wt_ref[...])
    logits = (lax.dot_general(wh, xh, nt, preferred_element_type=F32)
              + lax.dot_general(wh, xl, nt, preferred_element_type=F32)
              + lax.dot_general(wl, xh, nt, preferred_element_type=F32)) + b_ref[...]
    eidx = lax.broadcasted_iota(jnp.int32, (n_exp, tr), 0)
    r = lax.broadcasted_iota(jnp.int32, (tr, tr), 0)
    c = lax.broadcasted_iota(jnp.int32, (tr, tr), 1)
    before = jnp.where(r < c, 1.0, 0.0).astype(BF16)
    work = logits
    vals, sels = [], []
    seen = base_sc[...]
    for kk in range(TOP_K):
        m = jnp.max(work, axis=0, keepdims=True)
        sel = jnp.min(jnp.where(work == m, eidx, n_exp), axis=0, keepdims=True)
        hit = eidx == sel
        work = jnp.where(hit, -jnp.inf, work)
        onehot = jnp.where(hit, 1.0, 0.0)
        prefix = jnp.dot(onehot.astype(BF16), before, preferred_element_type=F32)
        rank = jnp.sum(onehot * (seen + prefix), axis=0, keepdims=True)
        rank_ref[kk:kk + 1, :] = rank.astype(jnp.int32)
        seen = seen + jnp.sum(onehot, axis=1, keepdims=True)
        vals.append(m)
        sels.append(sel)
    base_sc[...] = seen
    cnt_ref[...] = jnp.broadcast_to(seen, cnt_ref.shape)
    es = [jnp.exp(v - vals[0]) for v in vals]
    tot = es[0] + es[1] + es[2] + es[3]
    for kk in range(TOP_K):
        idx_ref[kk:kk + 1, :] = sels[kk]
        gate_ref[kk:kk + 1, :] = es[kk] / tot


def moe_router(x, w_router, b_router):
    n, d = x.shape
    n_exp = w_router.shape[1]
    tr = _tile(n, ROUTER_TILE_TARGET, LANES)
    tok = pl.BlockSpec((TOP_K, tr), lambda i: (0, i))
    idx, gates, rank, cnt = pl.pallas_call(
        _router_kernel,
        grid=(n // tr,),
        in_specs=[pl.BlockSpec((tr, d), lambda i: (i, 0)),
                  pl.BlockSpec((n_exp, d), lambda i: (0, 0)),
                  pl.BlockSpec((n_exp, 1), lambda i: (0, 0))],
        out_specs=[tok, tok, tok, pl.BlockSpec((n_exp, LANES), lambda i: (0, 0))],
        out_shape=[jax.ShapeDtypeStruct((TOP_K, n), jnp.int32),
                   jax.ShapeDtypeStruct((TOP_K, n), F32),
                   jax.ShapeDtypeStruct((TOP_K, n), jnp.int32),
                   jax.ShapeDtypeStruct((n_exp, LANES), F32)],
        scratch_shapes=[pltpu.VMEM((n_exp, 1), F32)],
        compiler_params=_params("arbitrary"),
        name="moe_router",
    )(x, w_router.T, b_router.reshape(n_exp, 1))
    return idx, gates, rank, cnt[:, 0].astype(jnp.int32)


def _deinterleave_matrix():
    r = lax.broadcasted_iota(jnp.int32, (MXU_DIM, MXU_DIM), 0)
    c = lax.broadcasted_iota(jnp.int32, (MXU_DIM, MXU_DIM), 1)
    src = jnp.where(c < MXU_DIM // 2, 2 * c, 2 * (c - MXU_DIM // 2) + 1)
    return jnp.where(r == src, 1.0, 0.0).astype(BF16)


def _expert_kernel(be_ref, nv_ref, x_ref, wu_ref, bu_ref, wd_ref, bd_ref, o_ref, wu_sc, wd_sc, glu_sc):
    i = pl.program_id(0)
    valid = i < nv_ref[0]
    fresh = (i == 0) | (be_ref[i] != be_ref[jnp.maximum(i - 1, 0)])
    n_chunks = wu_sc.shape[1] // MXU_DIM
    half = MXU_DIM // 2

    @pl.when(valid & fresh)
    def _():
        perm = _deinterleave_matrix()
        for cc in range(n_chunks):
            cols = slice(cc * MXU_DIM, (cc + 1) * MXU_DIM)
            wu_sc[:, cols] = jnp.dot(wu_ref[:, cols].astype(BF16), perm,
                                     preferred_element_type=F32).astype(BF16)
        wd_sc[...] = wd_ref[...].astype(BF16)

    @pl.when(valid)
    def _():
        xb = x_ref[...]
        for cc in range(n_chunks):
            cols = slice(cc * MXU_DIM, (cc + 1) * MXU_DIM)
            h = jnp.dot(xb, wu_sc[:, cols], preferred_element_type=F32) + bu_ref[:, cols]
            gate = jnp.minimum(h[:, :half], SWIGLU_LIMIT)
            up = jnp.clip(h[:, half:], -SWIGLU_LIMIT, SWIGLU_LIMIT)
            glu = gate * (1.0 / (1.0 + jnp.exp(-SWIGLU_ALPHA * gate)))
            glu_sc[:, cc * half:(cc + 1) * half] = ((up + 1.0) * glu).astype(BF16)
        o_ref[...] = jnp.dot(glu_sc[...], wd_sc[...], preferred_element_type=F32) + bd_ref[...]

    @pl.when(jnp.logical_not(valid))
    def _():
        o_ref[...] = jnp.zeros_like(o_ref)


def moe_experts(xs, block_e, n_valid, w_up, b_up_perm, w_down, b_down, layer):
    n_rows, d = xs.shape
    bm = MOE_BLOCK_ROWS
    n_exp, _, two_f = w_up.shape[1:]
    f = two_f // 2
    assert two_f % MXU_DIM == 0
    return pl.pallas_call(
        _expert_kernel,
        grid_spec=pltpu.PrefetchScalarGridSpec(
            num_scalar_prefetch=2,
            grid=(n_rows // bm,),
            in_specs=[pl.BlockSpec((bm, d), lambda i, be, nv: (i, 0)),
                      pl.BlockSpec((None, None, d, two_f), lambda i, be, nv: (layer, be[i], 0, 0)),
                      pl.BlockSpec((None, None, 1, two_f), lambda i, be, nv: (layer, be[i], 0, 0)),
                      pl.BlockSpec((None, None, f, d), lambda i, be, nv: (layer, be[i], 0, 0)),
                      pl.BlockSpec((None, None, 1, d), lambda i, be, nv: (layer, be[i], 0, 0))],
            out_specs=pl.BlockSpec((bm, d), lambda i, be, nv: (i, 0)),
            scratch_shapes=[pltpu.VMEM((d, two_f), BF16), pltpu.VMEM((f, d), BF16),
                            pltpu.VMEM((bm, f), BF16)]),
        out_shape=jax.ShapeDtypeStruct((n_rows, d), F32),
        compiler_params=_params("arbitrary"),
        name="moe_experts",
    )(block_e, n_valid, xs, w_up, b_up_perm, w_down, b_down)


def _deinterleave_bias(b_up):
    l, e, two_f = b_up.shape
    b = b_up.reshape(l, e, two_f // MXU_DIM, MXU_DIM // 2, 2)
    return jnp.swapaxes(b, -1, -2).reshape(l, e, 1, two_f)


def moe_ffn(x, xb, layer, w_router, b_router, w_up, b_up_perm, w_down, b_down):
    n, d = x.shape
    n_exp = w_router.shape[1]
    bm = MOE_BLOCK_ROWS
    idx, gates, rank, counts = moe_router(x, w_router, b_router)
    n_as = n * TOP_K
    n_blocks = -(-(n_as + n_exp * (bm - 1)) // bm)
    padded = (counts + bm - 1) // bm * bm
    pad_end = jnp.cumsum(padded)
    pad_start = pad_end - padded
    n_valid = pad_end[-1] // bm
    blk_start = jnp.minimum(jnp.arange(n_blocks, dtype=jnp.int32), n_valid - 1) * bm
    block_e = jnp.minimum(jnp.searchsorted(pad_end, blk_start, side='right'),
                          n_exp - 1).astype(jnp.int32)
    dest = pad_start[idx] + rank
    tok = jnp.broadcast_to(jnp.arange(n, dtype=jnp.int32)[None, :], (TOP_K, n))
    rows = jnp.zeros((n_blocks * bm,), jnp.int32).at[dest.reshape(-1)].set(tok.reshape(-1))
    xs = jnp.take(xb, rows, axis=0)
    ys = moe_experts(xs, block_e, n_valid.reshape(1).astype(jnp.int32), w_up, b_up_perm,
                     w_down, b_down.reshape(b_down.shape[0], n_exp, 1, d), layer)
    return jnp.sum(jnp.take(ys, dest, axis=0) * gates[:, :, None], axis=0)


def kernel(x_prompt, x_sample, cache_k, cache_v, state_ret, page_table, w_sb_qkv, w_sb_out, sb_bias,
           w_ret_in, w_ret_out, ln_mix_g, ln_mix_b, ln_ffn_g, ln_ffn_b, w_router, b_router,
           w_exp_up, b_exp_up, w_exp_down, b_exp_down):
    n_seq, t_len, d = x_prompt.shape
    dec_batch, dec_seq, _ = x_sample.shape
    depth = ln_mix_g.shape[0]
    page, sb_heads, sb_hd = cache_k.shape[2:]
    ret_heads, dk, dv = state_ret.shape[2:]
    past_len = page_table.shape[1] * page
    alpha = (2 * depth) ** 0.25
    n_prompt = n_seq * t_len
    n_sample = dec_batch * SAMPLE_PAD
    assert dec_seq <= SAMPLE_PAD
    sb_scale = sb_hd ** -0.5
    assert math.log2(sb_scale) == round(math.log2(sb_scale))

    xs_pad = jnp.pad(x_sample, ((0, 0), (0, SAMPLE_PAD - dec_seq), (0, 0)))
    x = jnp.concatenate([x_prompt.reshape(n_prompt, d), xs_pad.reshape(n_sample, d)], axis=0)
    n = n_prompt + n_sample

    half = dk // 2
    inv = ROPE_BASE ** (-jnp.arange(half, dtype=F32) / half)
    ang_p = jnp.arange(t_len, dtype=F32)[:, None] * inv[None, :]
    ang_s = (past_len + jnp.arange(RET_DECODE_ROWS, dtype=F32))[:, None] * inv[None, :]
    b_up_perm = _deinterleave_bias(b_exp_up)

    new_k, new_v, new_s = [], [], []
    for i in range(depth):
        j = i // 2
        if i % 2 == 0:
            qkv = stacked_proj(x, w_sb_qkv[j].astype(BF16), 3, sb_scale)
            o = jnp.zeros((n, d), BF16)
            o = sb_decode_attention(qkv, sb_bias[j], cache_k, cache_v, page_table, j, o,
                                    n_prompt, sb_heads, sb_hd, dec_seq)
            o = sb_prompt_attention(qkv, sb_bias[j], o, n_seq, t_len, sb_hd)
            w_out = w_sb_out[j]
            new_k.append(qkv[1])
            new_v.append(qkv[2])
        else:
            p = stacked_proj(x, w_ret_in[j].astype(BF16), 3)
            o = jnp.zeros((n, ret_heads * dv), BF16)
            o, s_s = retention_decode(p, jnp.cos(ang_s), jnp.sin(ang_s), state_ret, j, o,
                                      n_prompt, ret_heads, dk, dv, dec_seq)
            o, s_p = retention_prompt(p, jnp.cos(ang_p), jnp.sin(ang_p), o, n_seq, t_len,
                                      ret_heads, dk, dv)
            w_out = w_ret_out[j]
            new_s.append((s_p, s_s))
        x, xb = out_proj_ln(o, w_out.astype(BF16), x, ln_mix_g[i], ln_mix_b[i], alpha, n_prompt, dec_seq)
        f = moe_ffn(x, xb, i, w_router[i], b_router[i], w_exp_up, b_up_perm,
                    w_exp_down, b_exp_down)
        x = residual_ln(x, f, ln_ffn_g[i], ln_ffn_b[i], alpha, n_prompt, dec_seq)

    def prompt_rows(a):
        return a[:n_prompt].reshape(n_seq, t_len, sb_heads, sb_hd)

    def sample_rows(a):
        return a[n_prompt:].reshape(dec_batch, SAMPLE_PAD, -1)[:, :dec_seq]

    y_prompt = x[:n_prompt].reshape(n_seq, t_len, d)
    y_sample = sample_rows(x)
    k_prompt = jnp.stack([prompt_rows(a) for a in new_k])
    v_prompt = jnp.stack([prompt_rows(a) for a in new_v])
    k_sample = jnp.stack([sample_rows(a).reshape(dec_batch, dec_seq, sb_heads, sb_hd) for a in new_k])
    v_sample = jnp.stack([sample_rows(a).reshape(dec_batch, dec_seq, sb_heads, sb_hd) for a in new_v])
    state_prompt = jnp.stack([s[0] for s in new_s])
    state_sample = jnp.stack([s[1] for s in new_s])
    return (y_prompt, y_sample, k_prompt, v_prompt, state_prompt, k_sample, v_sample, state_sample)
```

```python
import functools
import math

import jax
import jax.numpy as jnp
from jax import lax
from jax.experimental import pallas as pl
from jax.experimental.pallas import tpu as pltpu
from jax.experimental.pallas import tpu_sc as plsc

F32 = jnp.float32
BF16 = jnp.bfloat16

TOP_K = 4
ROPE_BASE = 10000.0
LN_EPS = 1e-5
SWIGLU_LIMIT = 7.0
SWIGLU_ALPHA = 1.702

LANES = 128
SUBLANES = 8
MXU_DIM = 256
SAMPLE_PAD = SUBLANES
VMEM_LIMIT_BYTES = 56 * 1024 * 1024
ROW_TILE_TARGET = 1024
MOE_BLOCK_ROWS = 256
SB_BLOCK = LANES
SB_QUERY_TILE = 256
SB_KEY_TILE = 512
RET_CHUNK = 256
RET_DECODE_ROWS = 128
ROUTER_TILE_TARGET = 640
DECODE_PAGES_PER_STEP = 8
SC_CORES = 2
SC_SUBCORES = 16
SC_GATHER_ROWS = 32


def _params(*sem):
    return pltpu.CompilerParams(dimension_semantics=sem, vmem_limit_bytes=VMEM_LIMIT_BYTES)


def _tile(n, target, unit):
    best = None
    for t in range(unit, min(n, target) + 1, unit):
        if n % t == 0:
            best = t
    assert best is not None, (n, target, unit)
    return best


def _layer_norm(y, g, b):
    mu = jnp.mean(y, axis=-1, keepdims=True)
    yc = y - mu
    var = jnp.mean(yc * yc, axis=-1, keepdims=True)
    return yc * lax.rsqrt(var + LN_EPS) * g + b


def _keep_rows(tile_idx, tm, n_prompt, dec_seq):
    row = tile_idx * tm + lax.broadcasted_iota(jnp.int32, (tm, 1), 0)
    pad = (row >= n_prompt) & (((row - n_prompt) & (SAMPLE_PAD - 1)) >= dec_seq)
    return jnp.logical_not(pad)


def _stacked_proj_kernel(x_ref, w_ref, o_ref, *, first_scale):
    y = jnp.dot(x_ref[...].astype(BF16), w_ref[...], preferred_element_type=F32)
    if first_scale != 1.0:
        y = y * jnp.where(pl.program_id(1) == 0, first_scale, 1.0)
    o_ref[...] = y


def stacked_proj(x, w_bf16, n_groups, first_scale=1.0):
    n, k = x.shape
    width = w_bf16.shape[1] // n_groups
    tm = _tile(n, ROW_TILE_TARGET, SUBLANES)
    return pl.pallas_call(
        functools.partial(_stacked_proj_kernel, first_scale=first_scale),
        grid=(n // tm, n_groups),
        in_specs=[pl.BlockSpec((tm, k), lambda i, j: (i, 0)),
                  pl.BlockSpec((k, width), lambda i, j: (0, j))],
        out_specs=pl.BlockSpec((None, tm, width), lambda i, j: (j, i, 0)),
        out_shape=jax.ShapeDtypeStruct((n_groups, n, width), F32),
        compiler_params=_params("parallel", "arbitrary"),
        name="stacked_proj",
    )(x, w_bf16)


def _out_proj_ln_kernel(a_ref, w_ref, x_ref, g_ref, b_ref, o_ref, *, alpha, n_prompt, dec_seq):
    h = jnp.dot(a_ref[...], w_ref[...], preferred_element_type=F32)
    y = _layer_norm(alpha * x_ref[...] + h, g_ref[...], b_ref[...])
    keep = _keep_rows(pl.program_id(0), o_ref.shape[0], n_prompt, dec_seq)
    o_ref[...] = jnp.where(keep, y, 0.0)


def out_proj_ln(a, w_bf16, x, g, b, alpha, n_prompt, dec_seq):
    n, k = a.shape
    d = w_bf16.shape[1]
    tm = _tile(n, ROW_TILE_TARGET, SUBLANES)
    return pl.pallas_call(
        functools.partial(_out_proj_ln_kernel, alpha=alpha, n_prompt=n_prompt, dec_seq=dec_seq),
        grid=(n // tm,),
        in_specs=[pl.BlockSpec((tm, k), lambda i: (i, 0)),
                  pl.BlockSpec((k, d), lambda i: (0, 0)),
                  pl.BlockSpec((tm, d), lambda i: (i, 0)),
                  pl.BlockSpec((1, d), lambda i: (0, 0)),
                  pl.BlockSpec((1, d), lambda i: (0, 0))],
        out_specs=pl.BlockSpec((tm, d), lambda i: (i, 0)),
        out_shape=jax.ShapeDtypeStruct((n, d), F32),
        compiler_params=_params("parallel"),
        name="out_proj_ln",
    )(a, w_bf16, x, g.reshape(1, d), b.reshape(1, d))


def _softplus(z):
    sign = jnp.uint32(0x80000000)
    neg_abs = lax.bitcast_convert_type(lax.bitcast_convert_type(z, jnp.uint32) | sign, F32)
    return jnp.maximum(z, 0.0) + jnp.log(1.0 + jnp.exp(neg_abs))


def _suffix_sum(sp, tri):
    hi = sp.astype(BF16)
    lo = (sp - hi.astype(F32)).astype(BF16)
    return jnp.dot(jnp.concatenate([hi, lo], axis=1), tri, preferred_element_type=F32)


def _tri_incl(n):
    r = lax.broadcasted_iota(jnp.int32, (2 * n, n), 0)
    c = lax.broadcasted_iota(jnp.int32, (2 * n, n), 1)
    return jnp.where(jnp.where(r < n, r, r - n) >= c, 1.0, 0.0).astype(BF16)


def _sb_tile(q2, kbs, vbs, bias_col, carry, tri, masks):
    nt = (((1,), (1,)), ((), ()))
    zs, incls = [], []
    for kb, mask in zip(kbs, masks):
        z = lax.dot_general(q2, kb, nt, preferred_element_type=F32) + bias_col
        sp = _softplus(z)
        if mask is not None:
            sp = jnp.where(mask, sp, 0.0)
        zs.append(z)
        incls.append(_suffix_sum(sp, tri))
    pv = None
    for u in reversed(range(len(kbs))):
        a = jnp.exp(zs[u] - incls[u] - carry)
        if masks[u] is not None:
            a = jnp.where(masks[u], a, 0.0)
        part = jnp.dot(a.astype(BF16), vbs[u], preferred_element_type=F32)
        pv = part if pv is None else pv + part
        carry = carry + incls[u][:, 0:1]
    return pv, carry


def _sb_prompt_kernel(bias_ref, q_ref, k_ref, v_ref, o_in_ref, o_ref, *, head_dim, tq, tk):
    del o_in_ref
    pair = pl.program_id(1)
    n_sub = tk // SB_BLOCK
    first = lax.broadcasted_iota(jnp.int32, (tq, LANES), 1) < head_dim
    rows2 = lax.broadcasted_iota(jnp.int32, (2 * tq, 1), 0)
    bias_col = jnp.where(rows2 < tq, bias_ref[2 * pair], bias_ref[2 * pair + 1])
    tri = _tri_incl(SB_BLOCK)
    r2 = lax.broadcasted_iota(jnp.int32, (2 * tq, SB_BLOCK), 0)
    c_minus_r = lax.broadcasted_iota(jnp.int32, (2 * tq, SB_BLOCK), 1) - jnp.where(r2 < tq, r2, r2 - tq)

    def load(ref, k0):
        return [ref[pl.ds(pl.multiple_of(k0 + u * SB_BLOCK, SB_BLOCK), SB_BLOCK), :].astype(BF16)
                for u in range(n_sub)]

    def q_block(i, _):
        q0 = pl.multiple_of(i * tq, tq)
        q = q_ref[pl.ds(q0, tq), :]
        q2 = jnp.concatenate([jnp.where(first, q, 0.0), jnp.where(first, 0.0, q)], axis=0).astype(BF16)
        n_full = q0 // tk
        k0 = pl.multiple_of(n_full * tk, tk)
        masks = [c_minus_r < (q0 - k0 - u * SB_BLOCK) for u in range(n_sub)]
        acc, carry = _sb_tile(q2, load(k_ref, k0), load(v_ref, k0), bias_col,
                              jnp.zeros((2 * tq, 1), F32), tri, masks)

        def k_tile(jj, state):
            acc, carry = state
            k0 = pl.multiple_of((n_full - 1 - jj) * tk, tk)
            pv, carry = _sb_tile(q2, load(k_ref, k0), load(v_ref, k0), bias_col, carry, tri,
                                 [None] * n_sub)
            return acc + pv, carry

        acc, _ = lax.fori_loop(0, n_full, k_tile, (acc, carry))
        o_ref[pl.ds(q0, tq), :] = jnp.where(first, acc[:tq], acc[tq:]).astype(o_ref.dtype)
        return 0

    lax.fori_loop(0, q_ref.shape[0] // tq, q_block, 0)


def sb_prompt_attention(qkv, bias, o_init, n_seq, t_len, head_dim):
    _, n, d = qkv.shape
    assert 2 * head_dim == LANES
    tk = min(SB_KEY_TILE, t_len)
    tq = min(SB_QUERY_TILE, tk)
    assert t_len % tk == 0 and tk % tq == 0 and tk % SB_BLOCK == 0
    part = lambda g: pl.BlockSpec((None, t_len, LANES), lambda b, p: (g, b, p))
    return pl.pallas_call(
        functools.partial(_sb_prompt_kernel, head_dim=head_dim, tq=tq, tk=tk),
        grid=(n_seq, d // LANES),
        in_specs=[pl.BlockSpec(memory_space=pltpu.SMEM), part(0), part(1), part(2),
                  pl.BlockSpec(memory_space=pl.ANY)],
        out_specs=pl.BlockSpec((t_len, LANES), lambda b, p: (b, p)),
        out_shape=jax.ShapeDtypeStruct((n, d), BF16),
        input_output_aliases={4: 0},
        compiler_params=_params("parallel", "parallel"),
        name="sb_prompt_attention",
    )(bias, qkv, qkv, qkv, o_init)


def _sb_decode_kernel(pt_ref, bias_ref, q_ref, kn_ref, vn_ref, *rest,
                      n_heads, head_dim, pages_per_step, dec_seq):
    del pt_ref
    g = pages_per_step
    k_refs, v_refs = rest[:g], rest[g:2 * g]
    o_in_ref, o_ref, acc_sc, carry_sc, qbd_sc, kpad_sc, vpad_sc = rest[2 * g:]
    del o_in_ref
    step = pl.program_id(1)
    n_rows, d = qbd_sc.shape
    page = kpad_sc.shape[0]
    tri = _tri_incl(page)
    lane_head = lax.broadcasted_iota(jnp.int32, (n_rows, d), 1) // head_dim
    row_head = lax.broadcasted_iota(jnp.int32, (n_rows, d), 0) & (n_heads - 1)
    own = lane_head == row_head
    bias_col = bias_ref[...]

    def page_update(kbs, vbs, masks):
        pv, carry = _sb_tile(qbd_sc[...], kbs, vbs, bias_col, carry_sc[...], tri, masks)
        acc_sc[...] += pv
        carry_sc[...] = carry

    @pl.when(step == 0)
    def _():
        q = q_ref[...]
        qb = jnp.broadcast_to(q[:, None, :], (SAMPLE_PAD, n_heads, d)).reshape(n_rows, d)
        qbd_sc[...] = jnp.where(own, qb, 0.0).astype(BF16)
        acc_sc[...] = jnp.zeros_like(acc_sc)
        carry_sc[...] = jnp.zeros_like(carry_sc)
        kpad_sc[...] = jnp.zeros_like(kpad_sc)
        vpad_sc[...] = jnp.zeros_like(vpad_sc)
        kpad_sc[0:SAMPLE_PAD, :] = kn_ref[...]
        vpad_sc[0:SAMPLE_PAD, :] = vn_ref[...]
        t_row = lax.broadcasted_iota(jnp.int32, (n_rows, page), 0) // n_heads
        s_col = lax.broadcasted_iota(jnp.int32, (n_rows, page), 1)
        page_update([kpad_sc[...].astype(BF16)], [vpad_sc[...].astype(BF16)],
                    [(s_col < t_row) & (s_col < dec_seq)])

    page_update([kr[...] for kr in reversed(k_refs)], [vr[...] for vr in reversed(v_refs)],
                [None] * g)

    @pl.when(step == pl.num_programs(1) - 1)
    def _():
        acc = jnp.where(own, acc_sc[...], 0.0).reshape(SAMPLE_PAD, n_heads, d)
        o_ref[...] = jnp.sum(acc, axis=1).astype(o_ref.dtype)


def sb_decode_attention(qkv, bias, ck, cv, page_table, layer, o_init,
                        n_prompt, n_heads, head_dim, dec_seq):
    _, n, d = qkv.shape
    page = ck.shape[2]
    n_seq, n_pages = page_table.shape
    g = _tile(n_pages, DECODE_PAGES_PER_STEP, 1)
    assert n_prompt % SAMPLE_PAD == 0 and n_heads & (n_heads - 1) == 0
    n_rows = SAMPLE_PAD * n_heads
    base = n_prompt // SAMPLE_PAD
    tok = pl.BlockSpec((SAMPLE_PAD, d), lambda b, s, pt: (base + b, 0))
    part = lambda gi: pl.BlockSpec((None, SAMPLE_PAD, d), lambda b, s, pt: (gi, base + b, 0))

    def page_spec(j):
        return pl.BlockSpec(
            (None, None, page, d),
            lambda b, s, pt: (layer, pt[b * n_pages + (n_pages - 1 - (s * g + j))], 0, 0))

    kv_specs = [page_spec(j) for j in range(g)]
    bias_rows = jnp.tile(bias, SAMPLE_PAD).reshape(n_rows, 1)
    return pl.pallas_call(
        functools.partial(_sb_decode_kernel, n_heads=n_heads, head_dim=head_dim,
                          pages_per_step=g, dec_seq=dec_seq),
        grid_spec=pltpu.PrefetchScalarGridSpec(
            num_scalar_prefetch=1,
            grid=(n_seq, n_pages // g),
            in_specs=[pl.BlockSpec((n_rows, 1), lambda b, s, pt: (0, 0)), part(0), part(1), part(2)]
                     + kv_specs + kv_specs + [pl.BlockSpec(memory_space=pl.ANY)],
            out_specs=tok,
            scratch_shapes=[pltpu.VMEM((n_rows, d), F32), pltpu.VMEM((n_rows, 1), F32),
                            pltpu.VMEM((n_rows, d), BF16), pltpu.VMEM((page, d), F32),
                            pltpu.VMEM((page, d), F32)]),
        out_shape=jax.ShapeDtypeStruct((n, d), BF16),
        input_output_aliases={5 + 2 * g: 0},
        compiler_params=_params("parallel", "arbitrary"),
        name="sb_decode_attention",
    )(page_table.reshape(-1), bias_rows, qkv, qkv, qkv, *([ck] * g), *([cv] * g), o_init)


def _rope(a, cos, sin):
    half = a.shape[-1] // 2
    a1, a2 = a[:, :half], a[:, half:]
    return jnp.concatenate([a1 * cos - a2 * sin, a1 * sin + a2 * cos], axis=-1)


def _retention_chunk(p_ref, cos, sin, s_sc, store, *, n_heads, dk, dv, valid):
    c_len = p_ref.shape[1]
    qk_w = n_heads * dk
    idx = lax.broadcasted_iota(jnp.int32, (c_len, 1), 0).astype(F32)
    rel = (lax.broadcasted_iota(jnp.int32, (c_len, c_len), 0)
           - lax.broadcasted_iota(jnp.int32, (c_len, c_len), 1)).astype(F32)
    live = lax.broadcasted_iota(jnp.int32, (c_len, 1), 0) < valid
    nt = (((1,), (1,)), ((), ()))
    tn = (((0,), (0,)), ((), ()))
    for h in range(n_heads):
        log_g = math.log1p(-2.0 ** (-5.0 - h))
        q = _rope(p_ref[0, :, h * dk:(h + 1) * dk], cos, sin)
        k = _rope(p_ref[0, :, qk_w + h * dk:qk_w + (h + 1) * dk], cos, sin) * (dk ** -0.5)
        k = jnp.where(live, k, 0.0)
        v = p_ref[1, :, h * dv:(h + 1) * dv].astype(BF16)
        gate = p_ref[2, :, h * dv:(h + 1) * dv]
        intra = jnp.where(rel >= 0.0, jnp.exp(log_g * jnp.maximum(rel, 0.0)), 0.0)
        q_dec = jnp.exp(log_g * (idx + 1.0))
        k_dec = jnp.exp(log_g * (valid - 1.0 - idx))
        qb = q.astype(BF16)
        sc = lax.dot_general(qb, k.astype(BF16), nt, preferred_element_type=F32) * intra
        s_old = s_sc[h]
        o = (jnp.dot(sc.astype(BF16), v, preferred_element_type=F32)
             + jnp.dot(qb, s_old.astype(BF16), preferred_element_type=F32) * q_dec)
        s_sc[h] = (math.exp(log_g * valid) * s_old
                   + lax.dot_general((k * k_dec).astype(BF16), v, tn, preferred_element_type=F32))
        mu = jnp.mean(o, axis=-1, keepdims=True)
        oc = o - mu
        var = jnp.mean(oc * oc, axis=-1, keepdims=True)
        on = oc * lax.rsqrt(var + LN_EPS)
        store(h, gate * (1.0 / (1.0 + jnp.exp(-gate))) * on)


def _ret_prompt_kernel(p_ref, cos_ref, sin_ref, y_in_ref, y_ref, s_out_ref, s_sc, *, n_heads, dk, dv):
    del y_in_ref
    c = pl.program_id(1)

    @pl.when(c == 0)
    def _():
        s_sc[...] = jnp.zeros_like(s_sc)

    def store(h, val):
        y_ref[:, h * dv:(h + 1) * dv] = val.astype(y_ref.dtype)

    _retention_chunk(p_ref, cos_ref[...], sin_ref[...], s_sc, store,
                     n_heads=n_heads, dk=dk, dv=dv, valid=p_ref.shape[1])

    @pl.when(c == pl.num_programs(1) - 1)
    def _():
        s_out_ref[...] = s_sc[...]


def retention_prompt(p, cos, sin, y_init, n_seq, t_len, n_heads, dk, dv):
    _, n, v_w = p.shape
    assert v_w == n_heads * dv == 2 * n_heads * dk
    chunk = _tile(t_len, RET_CHUNK, SUBLANES)
    n_chunks = t_len // chunk
    tab = pl.BlockSpec((chunk, dk // 2), lambda b, c: (c, 0))
    return pl.pallas_call(
        functools.partial(_ret_prompt_kernel, n_heads=n_heads, dk=dk, dv=dv),
        grid=(n_seq, n_chunks),
        in_specs=[pl.BlockSpec((3, chunk, v_w), lambda b, c: (0, b * n_chunks + c, 0)),
                  tab, tab, pl.BlockSpec(memory_space=pl.ANY)],
        out_specs=[pl.BlockSpec((chunk, v_w), lambda b, c: (b * n_chunks + c, 0)),
                   pl.BlockSpec((None, n_heads, dk, dv), lambda b, c: (b, 0, 0, 0))],
        out_shape=[jax.ShapeDtypeStruct((n, v_w), BF16),
                   jax.ShapeDtypeStruct((n_seq, n_heads, dk, dv), F32)],
        scratch_shapes=[pltpu.VMEM((n_heads, dk, dv), F32)],
        input_output_aliases={3: 0},
        compiler_params=_params("parallel", "arbitrary"),
        name="retention_prompt",
    )(p, cos, sin, y_init)


def _ret_decode_kernel(p_ref, cos_ref, sin_ref, s_in_ref, y_in_ref, y_ref, s_out_ref, p_sc, s_sc,
                       *, n_heads, dk, dv, dec_seq):
    del y_in_ref
    p_sc[...] = jnp.zeros_like(p_sc)
    p_sc[:, 0:SAMPLE_PAD, :] = p_ref[...]
    s_sc[...] = s_in_ref[...]

    def store(h, val):
        y_ref[:, h * dv:(h + 1) * dv] = val[0:SAMPLE_PAD].astype(y_ref.dtype)

    _retention_chunk(p_sc, cos_ref[...], sin_ref[...], s_sc, store,
                     n_heads=n_heads, dk=dk, dv=dv, valid=dec_seq)
    s_out_ref[...] = s_sc[...]


def retention_decode(p, cos, sin, state, layer, y_init, n_prompt, n_heads, dk, dv, dec_seq):
    _, n, v_w = p.shape
    n_seq = state.shape[1]
    base = n_prompt // SAMPLE_PAD
    tab = pl.BlockSpec((RET_DECODE_ROWS, dk // 2), lambda b: (0, 0))
    return pl.pallas_call(
        functools.partial(_ret_decode_kernel, n_heads=n_heads, dk=dk, dv=dv, dec_seq=dec_seq),
        grid=(n_seq,),
        in_specs=[pl.BlockSpec((3, SAMPLE_PAD, v_w), lambda b: (0, base + b, 0)), tab, tab,
                  pl.BlockSpec((None, None, n_heads, dk, dv), lambda b: (layer, b, 0, 0, 0)),
                  pl.BlockSpec(memory_space=pl.ANY)],
        out_specs=[pl.BlockSpec((SAMPLE_PAD, v_w), lambda b: (base + b, 0)),
                   pl.BlockSpec((None, n_heads, dk, dv), lambda b: (b, 0, 0, 0))],
        out_shape=[jax.ShapeDtypeStruct((n, v_w), BF16),
                   jax.ShapeDtypeStruct((n_seq, n_heads, dk, dv), F32)],
        scratch_shapes=[pltpu.VMEM((3, RET_DECODE_ROWS, v_w), F32),
                        pltpu.VMEM((n_heads, dk, dv), F32)],
        input_output_aliases={4: 0},
        compiler_params=_params("parallel"),
        name="retention_decode",
    )(p, cos, sin, state, y_init)


def _split_bf16(a):
    hi = a.astype(BF16)
    return hi, (a - hi.astype(F32)).astype(BF16)


def _router_kernel(x_ref, wt_ref, b_ref, idx_ref, gate_ref, rank_ref, cnt_ref, base_sc):
    n_exp = wt_ref.shape[0]
    tr = x_ref.shape[0]

    @pl.when(pl.program_id(0) == 0)
    def _():
        base_sc[...] = jnp.zeros_like(base_sc)

    nt = (((1,), (1,)), ((), ()))
    xh, xl = _split_bf16(x_ref[...])
    wh, wl = _split_bf16(wt_ref[...])
    logits = (lax.dot_general(wh, xh, nt, preferred_element_type=F32)
              + lax.dot_general(wh, xl, nt, preferred_element_type=F32)
              + lax.dot_general(wl, xh, nt, preferred_element_type=F32)) + b_ref[...]
    eidx = lax.broadcasted_iota(jnp.int32, (n_exp, tr), 0)
    r = lax.broadcasted_iota(jnp.int32, (tr, tr), 0)
    c = lax.broadcasted_iota(jnp.int32, (tr, tr), 1)
    before = jnp.where(r < c, 1.0, 0.0).astype(BF16)
    work = logits
    vals, sels = [], []
    seen = base_sc[...]
    for kk in range(TOP_K):
        m = jnp.max(work, axis=0, keepdims=True)
        sel = jnp.min(jnp.where(work == m, eidx, n_exp), axis=0, keepdims=True)
        hit = eidx == sel
        work = jnp.where(hit, -jnp.inf, work)
        onehot = jnp.where(hit, 1.0, 0.0)
        prefix = jnp.dot(onehot.astype(BF16), before, preferred_element_type=F32)
        rank = jnp.sum(onehot * (seen + prefix), axis=0, keepdims=True)
        rank_ref[kk:kk + 1, :] = rank.astype(jnp.int32)
        seen = seen + jnp.sum(onehot, axis=1, keepdims=True)
        vals.append(m)
        sels.append(sel)
    base_sc[...] = seen
    cnt_ref[...] = jnp.broadcast_to(seen, cnt_ref.shape)
    es = [jnp.exp(v - vals[0]) for v in vals]
    tot = es[0] + es[1] + es[2] + es[3]
    for kk in range(TOP_K):
        idx_ref[kk:kk + 1, :] = sels[kk]
        gate_ref[kk:kk + 1, :] = es[kk] / tot


def moe_router(x, w_router, b_router):
    n, d = x.shape
    n_exp = w_router.shape[1]
    tr = _tile(n, ROUTER_TILE_TARGET, LANES)
    tok = pl.BlockSpec((TOP_K, tr), lambda i: (0, i))
    idx, gates, rank, cnt = pl.pallas_call(
        _router_kernel,
        grid=(n // tr,),
        in_specs=[pl.BlockSpec((tr, d), lambda i: (i, 0)),
                  pl.BlockSpec((n_exp, d), lambda i: (0, 0)),
                  pl.BlockSpec((n_exp, 1), lambda i: (0, 0))],
        out_specs=[tok, tok, tok, pl.BlockSpec((n_exp, LANES), lambda i: (0, 0))],
        out_shape=[jax.ShapeDtypeStruct((TOP_K, n), jnp.int32),
                   jax.ShapeDtypeStruct((TOP_K, n), F32),
                   jax.ShapeDtypeStruct((TOP_K, n), jnp.int32),
                   jax.ShapeDtypeStruct((n_exp, LANES), F32)],
        scratch_shapes=[pltpu.VMEM((n_exp, 1), F32)],
        compiler_params=_params("arbitrary"),
        name="moe_router",
    )(x, w_router.T, b_router.reshape(n_exp, 1))
    return idx, gates, rank, cnt[:, 0].astype(jnp.int32)


def _deinterleave_matrix():
    r = lax.broadcasted_iota(jnp.int32, (MXU_DIM, MXU_DIM), 0)
    c = lax.broadcasted_iota(jnp.int32, (MXU_DIM, MXU_DIM), 1)
    src = jnp.where(c < MXU_DIM // 2, 2 * c, 2 * (c - MXU_DIM // 2) + 1)
    return jnp.where(r == src, 1.0, 0.0).astype(BF16)


def _expert_kernel(be_ref, nv_ref, x_ref, wu_ref, bu_ref, wd_ref, bd_ref, o_ref, wu_sc, wd_sc, glu_sc):
    i = pl.program_id(0)
    valid = i < nv_ref[0]
    fresh = (i == 0) | (be_ref[i] != be_ref[jnp.maximum(i - 1, 0)])
    n_chunks = wu_sc.shape[1] // MXU_DIM
    half = MXU_DIM // 2

    @pl.when(valid & fresh)
    def _():
        perm = _deinterleave_matrix()
        for cc in range(n_chunks):
            cols = slice(cc * MXU_DIM, (cc + 1) * MXU_DIM)
            wu_sc[:, cols] = jnp.dot(wu_ref[:, cols].astype(BF16), perm,
                                     preferred_element_type=F32).astype(BF16)
        wd_sc[...] = wd_ref[...].astype(BF16)

    @pl.when(valid)
    def _():
        xb = x_ref[...].astype(BF16)
        for cc in range(n_chunks):
            cols = slice(cc * MXU_DIM, (cc + 1) * MXU_DIM)
            h = jnp.dot(xb, wu_sc[:, cols], preferred_element_type=F32) + bu_ref[:, cols]
            gate = jnp.minimum(h[:, :half], SWIGLU_LIMIT)
            up = jnp.clip(h[:, half:], -SWIGLU_LIMIT, SWIGLU_LIMIT)
            glu = gate * (1.0 / (1.0 + jnp.exp(-SWIGLU_ALPHA * gate)))
            glu_sc[:, cc * half:(cc + 1) * half] = ((up + 1.0) * glu).astype(BF16)
        o_ref[...] = jnp.dot(glu_sc[...], wd_sc[...], preferred_element_type=F32) + bd_ref[...]

    @pl.when(jnp.logical_not(valid))
    def _():
        o_ref[...] = jnp.zeros_like(o_ref)


def moe_experts(xs, block_e, n_valid, w_up, b_up_perm, w_down, b_down, layer):
    n_rows, d = xs.shape
    bm = MOE_BLOCK_ROWS
    n_exp, _, two_f = w_up.shape[1:]
    f = two_f // 2
    assert two_f % MXU_DIM == 0
    return pl.pallas_call(
        _expert_kernel,
        grid_spec=pltpu.PrefetchScalarGridSpec(
            num_scalar_prefetch=2,
            grid=(n_rows // bm,),
            in_specs=[pl.BlockSpec((bm, d), lambda i, be, nv: (i, 0)),
                      pl.BlockSpec((None, None, d, two_f), lambda i, be, nv: (layer, be[i], 0, 0)),
                      pl.BlockSpec((None, None, 1, two_f), lambda i, be, nv: (layer, be[i], 0, 0)),
                      pl.BlockSpec((None, None, f, d), lambda i, be, nv: (layer, be[i], 0, 0)),
                      pl.BlockSpec((None, None, 1, d), lambda i, be, nv: (layer, be[i], 0, 0))],
            out_specs=pl.BlockSpec((bm, d), lambda i, be, nv: (i, 0)),
            scratch_shapes=[pltpu.VMEM((d, two_f), BF16), pltpu.VMEM((f, d), BF16),
                            pltpu.VMEM((bm, f), BF16)]),
        out_shape=jax.ShapeDtypeStruct((n_rows, d), F32),
        compiler_params=_params("arbitrary"),
        name="moe_experts",
    )(block_e, n_valid, xs, w_up, b_up_perm, w_down, b_down)


def _deinterleave_bias(b_up):
    l, e, two_f = b_up.shape
    b = b_up.reshape(l, e, two_f // MXU_DIM, MXU_DIM // 2, 2)
    return jnp.swapaxes(b, -1, -2).reshape(l, e, 1, two_f)


def sc_gather_rows(table, idx):
    n_idx = idx.shape[0]
    _, d = table.shape
    n_workers = SC_CORES * SC_SUBCORES
    assert n_idx % (n_workers * SUBLANES) == 0
    per_worker = n_idx // n_workers
    chunk = _tile(per_worker, SC_GATHER_ROWS, SUBLANES)
    mesh = plsc.VectorSubcoreMesh(core_axis_name="c", subcore_axis_name="s",
                                  num_cores=SC_CORES, num_subcores=SC_SUBCORES)

    @functools.partial(
        pl.kernel, mesh=mesh,
        out_type=jax.ShapeDtypeStruct((n_idx, d), table.dtype),
        scratch_types=[pltpu.VMEM((chunk,), jnp.int32), pltpu.VMEM((chunk, d), table.dtype),
                       pltpu.SemaphoreType.DMA])
    def gather(table_hbm, idx_hbm, out_hbm, idx_v, rows_v, sem):
        worker = lax.axis_index("s") * SC_CORES + lax.axis_index("c")
        base = worker * per_worker

        @pl.loop(0, per_worker // chunk)
        def _(c):
            off = pl.multiple_of(base + c * chunk, SUBLANES)
            pltpu.sync_copy(idx_hbm.at[pl.ds(off, chunk)], idx_v)
            pltpu.async_copy(table_hbm.at[idx_v], rows_v, sem).wait()
            pltpu.sync_copy(rows_v, out_hbm.at[pl.ds(off, chunk)])

    return gather(table, idx)


def _combine_ln_kernel(x_ref, y_ref, gate_ref, g_ref, b_ref, o_ref, *, alpha, n_prompt, dec_seq):
    gates = gate_ref[...]
    f = gates[:, 0:1] * y_ref[0]
    for kk in range(1, TOP_K):
        f = f + gates[:, kk:kk + 1] * y_ref[kk]
    y = _layer_norm(alpha * x_ref[...] + f, g_ref[...], b_ref[...])
    keep = _keep_rows(pl.program_id(0), o_ref.shape[0], n_prompt, dec_seq)
    o_ref[...] = jnp.where(keep, y, 0.0)


def combine_ln(x, y4, gates_t, g, b, alpha, n_prompt, dec_seq):
    n, d = x.shape
    tm = _tile(n, ROW_TILE_TARGET // 2, SUBLANES)
    row = pl.BlockSpec((tm, d), lambda i: (i, 0))
    vec = pl.BlockSpec((1, d), lambda i: (0, 0))
    return pl.pallas_call(
        functools.partial(_combine_ln_kernel, alpha=alpha, n_prompt=n_prompt, dec_seq=dec_seq),
        grid=(n // tm,),
        in_specs=[row, pl.BlockSpec((TOP_K, tm, d), lambda i: (0, i, 0)),
                  pl.BlockSpec((tm, TOP_K), lambda i: (i, 0)), vec, vec],
        out_specs=row,
        out_shape=jax.ShapeDtypeStruct((n, d), F32),
        compiler_params=_params("parallel"),
        name="combine_ln",
    )(x, y4, gates_t, g.reshape(1, d), b.reshape(1, d))


def moe_ffn_ln(x, layer, w_router, b_router, w_up, b_up_perm, w_down, b_down, g, b,
               alpha, n_prompt, dec_seq):
    n, d = x.shape
    n_exp = w_router.shape[1]
    bm = MOE_BLOCK_ROWS
    idx, gates, rank, counts = moe_router(x, w_router, b_router)
    n_blocks = -(-(n * TOP_K + n_exp * (bm - 1)) // bm)
    padded = (counts + bm - 1) // bm * bm
    pad_end = jnp.cumsum(padded)
    pad_start = pad_end - padded
    n_valid = pad_end[-1] // bm
    blk_start = jnp.minimum(jnp.arange(n_blocks, dtype=jnp.int32), n_valid - 1) * bm
    block_e = jnp.minimum(jnp.sum(pad_end[None, :] <= blk_start[:, None], axis=1),
                          n_exp - 1).astype(jnp.int32)
    experts = jnp.arange(n_exp, dtype=jnp.int32)[:, None, None]
    dest = rank + jnp.sum(jnp.where(idx[None] == experts, pad_start[:, None, None], 0), axis=0)
    tok = jnp.broadcast_to(jnp.arange(n, dtype=jnp.int32)[None, :], (TOP_K, n))
    rows = jnp.zeros((n_blocks * bm,), jnp.int32).at[dest.reshape(-1)].set(tok.reshape(-1))
    xs = sc_gather_rows(x, rows)
    ys = moe_experts(xs, block_e, n_valid.reshape(1).astype(jnp.int32), w_up, b_up_perm,
                     w_down, b_down.reshape(b_down.shape[0], n_exp, 1, d), layer)
    y4 = sc_gather_rows(ys, dest.reshape(-1)).reshape(TOP_K, n, d)
    return combine_ln(x, y4, gates.T, g, b, alpha, n_prompt, dec_seq)


def kernel(x_prompt, x_sample, cache_k, cache_v, state_ret, page_table, w_sb_qkv, w_sb_out, sb_bias,
           w_ret_in, w_ret_out, ln_mix_g, ln_mix_b, ln_ffn_g, ln_ffn_b, w_router, b_router,
           w_exp_up, b_exp_up, w_exp_down, b_exp_down):
    n_seq, t_len, d = x_prompt.shape
    dec_batch, dec_seq, _ = x_sample.shape
    depth = ln_mix_g.shape[0]
    page, sb_heads, sb_hd = cache_k.shape[2:]
    ret_heads, dk, dv = state_ret.shape[2:]
    past_len = page_table.shape[1] * page
    alpha = (2 * depth) ** 0.25
    n_prompt = n_seq * t_len
    n_sample = dec_batch * SAMPLE_PAD
    assert dec_seq <= SAMPLE_PAD
    sb_scale = sb_hd ** -0.5
    assert math.log2(sb_scale) == round(math.log2(sb_scale))

    xs_pad = jnp.pad(x_sample, ((0, 0), (0, SAMPLE_PAD - dec_seq), (0, 0)))
    x = jnp.concatenate([x_prompt.reshape(n_prompt, d), xs_pad.reshape(n_sample, d)], axis=0)
    n = n_prompt + n_sample

    half = dk // 2
    inv = ROPE_BASE ** (-jnp.arange(half, dtype=F32) / half)
    ang_p = jnp.arange(t_len, dtype=F32)[:, None] * inv[None, :]
    ang_s = (past_len + jnp.arange(RET_DECODE_ROWS, dtype=F32))[:, None] * inv[None, :]
    b_up_perm = _deinterleave_bias(b_exp_up)
    cache_shape = cache_k.shape[:3] + (sb_heads * sb_hd,)
    ck = cache_k.astype(BF16).reshape(cache_shape)
    cv = cache_v.astype(BF16).reshape(cache_shape)

    new_k, new_v, new_s = [], [], []
    for i in range(depth):
        j = i // 2
        if i % 2 == 0:
            qkv = stacked_proj(x, w_sb_qkv[j].astype(BF16), 3, sb_scale)
            o = jnp.zeros((n, d), BF16)
            o = sb_decode_attention(qkv, sb_bias[j], ck, cv, page_table, j, o,
                                    n_prompt, sb_heads, sb_hd, dec_seq)
            o = sb_prompt_attention(qkv, sb_bias[j], o, n_seq, t_len, sb_hd)
            w_out = w_sb_out[j]
            new_k.append(qkv[1])
            new_v.append(qkv[2])
        else:
            p = stacked_proj(x, w_ret_in[j].astype(BF16), 3)
            o = jnp.zeros((n, ret_heads * dv), BF16)
            o, s_s = retention_decode(p, jnp.cos(ang_s), jnp.sin(ang_s), state_ret, j, o,
                                      n_prompt, ret_heads, dk, dv, dec_seq)
            o, s_p = retention_prompt(p, jnp.cos(ang_p), jnp.sin(ang_p), o, n_seq, t_len,
                                      ret_heads, dk, dv)
            w_out = w_ret_out[j]
            new_s.append((s_p, s_s))
        x = out_proj_ln(o, w_out.astype(BF16), x, ln_mix_g[i], ln_mix_b[i], alpha, n_prompt, dec_seq)
        x = moe_ffn_ln(x, i, w_router[i], b_router[i], w_exp_up, b_up_perm, w_exp_down, b_exp_down,
                       ln_ffn_g[i], ln_ffn_b[i], alpha, n_prompt, dec_seq)

    def prompt_rows(a):
        return a[:n_prompt].reshape(n_seq, t_len, sb_heads, sb_hd)

    def sample_rows(a):
        return a[n_prompt:].reshape(dec_batch, SAMPLE_PAD, -1)[:, :dec_seq]

    y_prompt = x[:n_prompt].reshape(n_seq, t_len, d)
    y_sample = sample_rows(x)
    k_prompt = jnp.stack([prompt_rows(a) for a in new_k])
    v_prompt = jnp.stack([prompt_rows(a) for a in new_v])
    k_sample = jnp.stack([sample_rows(a).reshape(dec_batch, dec_seq, sb_heads, sb_hd) for a in new_k])
    v_sample = jnp.stack([sample_rows(a).reshape(dec_batch, dec_seq, sb_heads, sb_hd) for a in new_v])
    state_prompt = jnp.stack([s[0] for s in new_s])
    state_sample = jnp.stack([s[1] for s in new_s])
    return (y_prompt, y_sample, k_prompt, v_prompt, state_prompt, k_sample, v_sample, state_sample)
```

```python
import functools
import math

import jax
import jax.numpy as jnp
from jax import lax
from jax.experimental import pallas as pl
from jax.experimental.pallas import tpu as pltpu
from jax.experimental.pallas import tpu_sc as plsc

F32 = jnp.float32
BF16 = jnp.bfloat16

TOP_K = 4
ROPE_BASE = 10000.0
LN_EPS = 1e-5
SWIGLU_LIMIT = 7.0
SWIGLU_ALPHA = 1.702

LANES = 128
SUBLANES = 8
MXU_DIM = 256
SAMPLE_PAD = SUBLANES
VMEM_LIMIT_BYTES = 56 * 1024 * 1024
ROW_TILE_TARGET = 1024
MOE_BLOCK_ROWS = 256
SB_BLOCK = LANES
SB_QUERY_TILE = 256
SB_KEY_TILE = 512
RET_CHUNK = 256
RET_DECODE_ROWS = 128
ROUTER_TILE_TARGET = 640
DECODE_PAGES_PER_STEP = 4
SC_CORES = 2
SC_SUBCORES = 16
SC_GATHER_ROWS = 32


def _params(*sem):
    return pltpu.CompilerParams(dimension_semantics=sem, vmem_limit_bytes=VMEM_LIMIT_BYTES)


def _tile(n, target, unit):
    best = None
    for t in range(unit, min(n, target) + 1, unit):
        if n % t == 0:
            best = t
    assert best is not None, (n, target, unit)
    return best


def _layer_norm(y, g, b):
    mu = jnp.mean(y, axis=-1, keepdims=True)
    yc = y - mu
    var = jnp.mean(yc * yc, axis=-1, keepdims=True)
    return yc * lax.rsqrt(var + LN_EPS) * g + b


def _keep_rows(tile_idx, tm, n_prompt, dec_seq):
    row = tile_idx * tm + lax.broadcasted_iota(jnp.int32, (tm, 1), 0)
    pad = (row >= n_prompt) & (((row - n_prompt) & (SAMPLE_PAD - 1)) >= dec_seq)
    return jnp.logical_not(pad)


def _stacked_proj_kernel(x_ref, w_ref, o_ref, *, first_scale):
    y = jnp.dot(x_ref[...].astype(BF16), w_ref[...], preferred_element_type=F32)
    if first_scale != 1.0:
        y = y * jnp.where(pl.program_id(1) == 0, first_scale, 1.0)
    o_ref[...] = y


def stacked_proj(x, w_bf16, n_groups, first_scale=1.0):
    n, k = x.shape
    width = w_bf16.shape[1] // n_groups
    tm = _tile(n, ROW_TILE_TARGET, SUBLANES)
    return pl.pallas_call(
        functools.partial(_stacked_proj_kernel, first_scale=first_scale),
        grid=(n // tm, n_groups),
        in_specs=[pl.BlockSpec((tm, k), lambda i, j: (i, 0)),
                  pl.BlockSpec((k, width), lambda i, j: (0, j))],
        out_specs=pl.BlockSpec((None, tm, width), lambda i, j: (j, i, 0)),
        out_shape=jax.ShapeDtypeStruct((n_groups, n, width), F32),
        compiler_params=_params("parallel", "arbitrary"),
        name="stacked_proj",
    )(x, w_bf16)


def _out_proj_ln_kernel(a_ref, w_ref, x_ref, g_ref, b_ref, o_ref, *, alpha, n_prompt, dec_seq):
    h = jnp.dot(a_ref[...], w_ref[...], preferred_element_type=F32)
    y = _layer_norm(alpha * x_ref[...] + h, g_ref[...], b_ref[...])
    keep = _keep_rows(pl.program_id(0), o_ref.shape[0], n_prompt, dec_seq)
    o_ref[...] = jnp.where(keep, y, 0.0)


def out_proj_ln(a, w_bf16, x, g, b, alpha, n_prompt, dec_seq):
    n, k = a.shape
    d = w_bf16.shape[1]
    tm = _tile(n, ROW_TILE_TARGET, SUBLANES)
    return pl.pallas_call(
        functools.partial(_out_proj_ln_kernel, alpha=alpha, n_prompt=n_prompt, dec_seq=dec_seq),
        grid=(n // tm,),
        in_specs=[pl.BlockSpec((tm, k), lambda i: (i, 0)),
                  pl.BlockSpec((k, d), lambda i: (0, 0)),
                  pl.BlockSpec((tm, d), lambda i: (i, 0)),
                  pl.BlockSpec((1, d), lambda i: (0, 0)),
                  pl.BlockSpec((1, d), lambda i: (0, 0))],
        out_specs=pl.BlockSpec((tm, d), lambda i: (i, 0)),
        out_shape=jax.ShapeDtypeStruct((n, d), F32),
        compiler_params=_params("parallel"),
        name="out_proj_ln",
    )(a, w_bf16, x, g.reshape(1, d), b.reshape(1, d))


def _softplus(z):
    sign = jnp.uint32(0x80000000)
    neg_abs = lax.bitcast_convert_type(lax.bitcast_convert_type(z, jnp.uint32) | sign, F32)
    return jnp.maximum(z, 0.0) + jnp.log(1.0 + jnp.exp(neg_abs))


def _suffix_sum(sp, tri):
    hi = sp.astype(BF16)
    lo = (sp - hi.astype(F32)).astype(BF16)
    return jnp.dot(jnp.concatenate([hi, lo], axis=1), tri, preferred_element_type=F32)


def _tri_incl(n):
    r = lax.broadcasted_iota(jnp.int32, (2 * n, n), 0)
    c = lax.broadcasted_iota(jnp.int32, (2 * n, n), 1)
    return jnp.where(jnp.where(r < n, r, r - n) >= c, 1.0, 0.0).astype(BF16)


def _sb_tile(q2, kbs, vbs, bias_col, carry, tri, masks):
    nt = (((1,), (1,)), ((), ()))
    zs, incls = [], []
    for kb, mask in zip(kbs, masks):
        z = lax.dot_general(q2, kb, nt, preferred_element_type=F32) + bias_col
        sp = _softplus(z)
        if mask is not None:
            sp = jnp.where(mask, sp, 0.0)
        zs.append(z)
        incls.append(_suffix_sum(sp, tri))
    pv = None
    for u in reversed(range(len(kbs))):
        a = jnp.exp(zs[u] - incls[u] - carry)
        if masks[u] is not None:
            a = jnp.where(masks[u], a, 0.0)
        part = jnp.dot(a.astype(BF16), vbs[u], preferred_element_type=F32)
        pv = part if pv is None else pv + part
        carry = carry + incls[u][:, 0:1]
    return pv, carry


def _sb_prompt_kernel(bias_ref, q_ref, k_ref, v_ref, o_in_ref, o_ref, *, head_dim, tq, tk):
    del o_in_ref
    pair = pl.program_id(1)
    n_sub = tk // SB_BLOCK
    first = lax.broadcasted_iota(jnp.int32, (tq, LANES), 1) < head_dim
    rows2 = lax.broadcasted_iota(jnp.int32, (2 * tq, 1), 0)
    bias_col = jnp.where(rows2 < tq, bias_ref[2 * pair], bias_ref[2 * pair + 1])
    tri = _tri_incl(SB_BLOCK)
    r2 = lax.broadcasted_iota(jnp.int32, (2 * tq, SB_BLOCK), 0)
    c_minus_r = lax.broadcasted_iota(jnp.int32, (2 * tq, SB_BLOCK), 1) - jnp.where(r2 < tq, r2, r2 - tq)

    def load(ref, k0, n_blocks=n_sub):
        return [ref[pl.ds(pl.multiple_of(k0 + u * SB_BLOCK, SB_BLOCK), SB_BLOCK), :].astype(BF16)
                for u in range(n_blocks)]

    def q_block(i, _):
        q0 = pl.multiple_of(i * tq, tq)
        q = q_ref[pl.ds(q0, tq), :]
        q2 = jnp.concatenate([jnp.where(first, q, 0.0), jnp.where(first, 0.0, q)], axis=0).astype(BF16)
        n_full = q0 // tk
        k0 = pl.multiple_of(n_full * tk, tk)

        def partial_tile(n_blocks):
            def run():
                masks = [c_minus_r < (q0 - k0 - u * SB_BLOCK) for u in range(n_blocks)]
                return _sb_tile(q2, load(k_ref, k0, n_blocks), load(v_ref, k0, n_blocks), bias_col,
                                jnp.zeros((2 * tq, 1), F32), tri, masks)
            return run

        if tq < tk:
            acc, carry = lax.cond(q0 == k0, partial_tile(tq // SB_BLOCK), partial_tile(n_sub))
        else:
            acc, carry = partial_tile(n_sub)()

        def k_tile(jj, state):
            acc, carry = state
            k0 = pl.multiple_of((n_full - 1 - jj) * tk, tk)
            pv, carry = _sb_tile(q2, load(k_ref, k0), load(v_ref, k0), bias_col, carry, tri,
                                 [None] * n_sub)
            return acc + pv, carry

        acc, _ = lax.fori_loop(0, n_full, k_tile, (acc, carry))
        o_ref[pl.ds(q0, tq), :] = jnp.where(first, acc[:tq], acc[tq:]).astype(o_ref.dtype)
        return 0

    lax.fori_loop(0, q_ref.shape[0] // tq, q_block, 0)


def sb_prompt_attention(qkv, bias, o_init, n_seq, t_len, head_dim):
    _, n, d = qkv.shape
    assert 2 * head_dim == LANES
    tk = min(SB_KEY_TILE, t_len)
    tq = min(SB_QUERY_TILE, tk)
    assert t_len % tk == 0 and tk % tq == 0 and tk % SB_BLOCK == 0
    part = lambda g: pl.BlockSpec((None, t_len, LANES), lambda b, p: (g, b, p))
    return pl.pallas_call(
        functools.partial(_sb_prompt_kernel, head_dim=head_dim, tq=tq, tk=tk),
        grid=(n_seq, d // LANES),
        in_specs=[pl.BlockSpec(memory_space=pltpu.SMEM), part(0), part(1), part(2),
                  pl.BlockSpec(memory_space=pl.ANY)],
        out_specs=pl.BlockSpec((t_len, LANES), lambda b, p: (b, p)),
        out_shape=jax.ShapeDtypeStruct((n, d), BF16),
        input_output_aliases={4: 0},
        compiler_params=_params("parallel", "parallel"),
        name="sb_prompt_attention",
    )(bias, qkv, qkv, qkv, o_init)


def _sb_decode_kernel(pt_ref, bias_ref, q_ref, kn_ref, vn_ref, *rest,
                             n_heads, pages_per_step, dec_seq):
    del pt_ref
    g = pages_per_step
    k_refs, v_refs = rest[:g], rest[g:2 * g]
    same_ref, own_ref, expand_ref, tri_ref, o_ref, acc_sc, carry_sc, kpad_sc, vpad_sc, zt_sc = rest[2 * g:]
    step = pl.program_id(1)
    page, _, hd = kpad_sc.shape
    n_q = SAMPLE_PAD * n_heads
    rows = page * n_heads
    log_h = n_heads.bit_length() - 1
    nt = (((1,), (1,)), ((), ()))
    q2 = q_ref[...].reshape(n_q, hd).astype(BF16)
    bias_row = bias_ref[...]
    tri = tri_ref[...]

    def page_update(k3s, v3s, masks):
        zts, incls = [], []
        for u, (k3, mask) in enumerate(zip(k3s, masks)):
            gm = lax.dot_general(k3.reshape(rows, hd).astype(BF16), q2, nt,
                                 preferred_element_type=F32)
            zt_sc[u] = jnp.sum((gm * same_ref[...]).reshape(page, n_heads, n_q), axis=1)
            zt = zt_sc[u] + bias_row
            sp = _softplus(zt)
            if mask is not None:
                sp = jnp.where(mask, sp, 0.0)
            hi = sp.astype(BF16)
            lo = (sp - hi.astype(F32)).astype(BF16)
            zts.append(zt)
            incls.append(jnp.dot(tri, jnp.concatenate([hi, lo], axis=0), preferred_element_type=F32))
        carry = carry_sc[...]
        acc = acc_sc[...]
        for u in reversed(range(len(k3s))):
            at = jnp.exp(zts[u] - incls[u] - carry)
            if masks[u] is not None:
                at = jnp.where(masks[u], at, 0.0)
            carry = carry + incls[u][0:1, :]
            spread = jnp.dot(at.T.astype(BF16), expand_ref[...], preferred_element_type=F32)
            a_big = spread.astype(BF16) * own_ref[...]
            acc = acc + jnp.dot(a_big, v3s[u].reshape(rows, hd).astype(BF16), preferred_element_type=F32)
        carry_sc[...] = carry
        acc_sc[...] = acc

    @pl.when(step == 0)
    def _():
        acc_sc[...] = jnp.zeros_like(acc_sc)
        carry_sc[...] = jnp.zeros_like(carry_sc)
        kpad_sc[...] = jnp.zeros_like(kpad_sc)
        vpad_sc[...] = jnp.zeros_like(vpad_sc)
        kpad_sc[0:SAMPLE_PAD] = kn_ref[...]
        vpad_sc[0:SAMPLE_PAD] = vn_ref[...]
        s_key = lax.broadcasted_iota(jnp.int32, (page, n_q), 0)
        t_query = lax.broadcasted_iota(jnp.int32, (page, n_q), 1) >> log_h
        page_update([kpad_sc[...]], [vpad_sc[...]], [(s_key < t_query) & (s_key < dec_seq)])

    page_update([kr[...] for kr in reversed(k_refs)], [vr[...] for vr in reversed(v_refs)], [None] * g)

    @pl.when(step == pl.num_programs(1) - 1)
    def _():
        o_ref[...] = acc_sc[...].reshape(SAMPLE_PAD, n_heads, hd)


def sb_decode_attention(qkv_s, bias, cache_k, cache_v, page_table, layer, dec_seq):
    _, n_s, n_heads, hd = qkv_s.shape
    page = cache_k.shape[2]
    n_seq, n_pages = page_table.shape
    g = _tile(n_pages, DECODE_PAGES_PER_STEP, 1)
    assert n_heads & (n_heads - 1) == 0 and n_s == n_seq * SAMPLE_PAD
    n_q = SAMPLE_PAD * n_heads
    part = lambda gi: pl.BlockSpec((None, SAMPLE_PAD, n_heads, hd), lambda b, s, pt: (gi, b, 0, 0))

    def page_spec(j):
        return pl.BlockSpec(
            (None, None, page, n_heads, hd),
            lambda b, s, pt: (layer, pt[b * n_pages + (n_pages - 1 - (s * g + j))], 0, 0, 0))

    kv_specs = [page_spec(j) for j in range(g)]
    rows = page * n_heads
    head_of_row = jnp.arange(rows, dtype=jnp.int32) % n_heads
    head_of_q = jnp.arange(n_q, dtype=jnp.int32) % n_heads
    same = (head_of_row[:, None] == head_of_q[None, :]).astype(F32)
    own = (head_of_q[:, None] == head_of_row[None, :]).astype(BF16)
    key = jnp.arange(page, dtype=jnp.int32)
    expand = (key[:, None] == (jnp.arange(rows, dtype=jnp.int32) // n_heads)[None, :]).astype(BF16)
    tri = (jnp.tile(key, 2)[None, :] >= key[:, None]).astype(BF16)
    const = lambda a: pl.BlockSpec(a.shape, lambda b, s, pt: (0, 0))
    return pl.pallas_call(
        functools.partial(_sb_decode_kernel, n_heads=n_heads, pages_per_step=g, dec_seq=dec_seq),
        grid_spec=pltpu.PrefetchScalarGridSpec(
            num_scalar_prefetch=1,
            grid=(n_seq, n_pages // g),
            in_specs=[pl.BlockSpec((1, n_q), lambda b, s, pt: (0, 0)), part(0), part(1), part(2)]
                     + kv_specs + kv_specs + [const(same), const(own), const(expand), const(tri)],
            out_specs=pl.BlockSpec((SAMPLE_PAD, n_heads, hd), lambda b, s, pt: (b, 0, 0)),
            scratch_shapes=[pltpu.VMEM((n_q, hd), F32), pltpu.VMEM((1, n_q), F32),
                            pltpu.VMEM((page, n_heads, hd), F32), pltpu.VMEM((page, n_heads, hd), F32),
                            pltpu.VMEM((g, page, n_q), F32)]),
        out_shape=jax.ShapeDtypeStruct((n_s, n_heads, hd), F32),
        compiler_params=_params("parallel", "arbitrary"),
        name="sb_decode_attention",
    )(page_table.reshape(-1), jnp.tile(bias, SAMPLE_PAD).reshape(1, n_q), qkv_s, qkv_s, qkv_s,
      *([cache_k] * g), *([cache_v] * g), same, own, expand, tri)


def _rope(a, cos, sin):
    half = a.shape[-1] // 2
    a1, a2 = a[:, :half], a[:, half:]
    return jnp.concatenate([a1 * cos - a2 * sin, a1 * sin + a2 * cos], axis=-1)


def _retention_chunk(p_ref, cos, sin, s_sc, store, *, n_heads, dk, dv, valid):
    c_len = p_ref.shape[1]
    qk_w = n_heads * dk
    idx = lax.broadcasted_iota(jnp.int32, (c_len, 1), 0).astype(F32)
    rel = (lax.broadcasted_iota(jnp.int32, (c_len, c_len), 0)
           - lax.broadcasted_iota(jnp.int32, (c_len, c_len), 1)).astype(F32)
    live = lax.broadcasted_iota(jnp.int32, (c_len, 1), 0) < valid
    nt = (((1,), (1,)), ((), ()))
    tn = (((0,), (0,)), ((), ()))
    for h in range(n_heads):
        log_g = math.log1p(-2.0 ** (-5.0 - h))
        q = _rope(p_ref[0, :, h * dk:(h + 1) * dk], cos, sin)
        k = _rope(p_ref[0, :, qk_w + h * dk:qk_w + (h + 1) * dk], cos, sin) * (dk ** -0.5)
        k = jnp.where(live, k, 0.0)
        v = p_ref[1, :, h * dv:(h + 1) * dv].astype(BF16)
        gate = p_ref[2, :, h * dv:(h + 1) * dv]
        intra = jnp.where(rel >= 0.0, jnp.exp(log_g * jnp.maximum(rel, 0.0)), 0.0)
        q_dec = jnp.exp(log_g * (idx + 1.0))
        k_dec = jnp.exp(log_g * (valid - 1.0 - idx))
        qb = q.astype(BF16)
        sc = lax.dot_general(qb, k.astype(BF16), nt, preferred_element_type=F32) * intra
        s_old = s_sc[h]
        o = (jnp.dot(sc.astype(BF16), v, preferred_element_type=F32)
             + jnp.dot(qb, s_old.astype(BF16), preferred_element_type=F32) * q_dec)
        s_sc[h] = (math.exp(log_g * valid) * s_old
                   + lax.dot_general((k * k_dec).astype(BF16), v, tn, preferred_element_type=F32))
        mu = jnp.mean(o, axis=-1, keepdims=True)
        oc = o - mu
        var = jnp.mean(oc * oc, axis=-1, keepdims=True)
        on = oc * lax.rsqrt(var + LN_EPS)
        store(h, gate * (1.0 / (1.0 + jnp.exp(-gate))) * on)


def _ret_prompt_kernel(p_ref, cos_ref, sin_ref, y_in_ref, y_ref, s_out_ref, s_sc, *, n_heads, dk, dv):
    del y_in_ref
    c = pl.program_id(1)

    @pl.when(c == 0)
    def _():
        s_sc[...] = jnp.zeros_like(s_sc)

    def store(h, val):
        y_ref[:, h * dv:(h + 1) * dv] = val.astype(y_ref.dtype)

    _retention_chunk(p_ref, cos_ref[...], sin_ref[...], s_sc, store,
                     n_heads=n_heads, dk=dk, dv=dv, valid=p_ref.shape[1])

    @pl.when(c == pl.num_programs(1) - 1)
    def _():
        s_out_ref[...] = s_sc[...]


def retention_prompt(p, cos, sin, y_init, n_seq, t_len, n_heads, dk, dv):
    _, n, v_w = p.shape
    assert v_w == n_heads * dv == 2 * n_heads * dk
    chunk = _tile(t_len, RET_CHUNK, SUBLANES)
    n_chunks = t_len // chunk
    tab = pl.BlockSpec((chunk, dk // 2), lambda b, c: (c, 0))
    return pl.pallas_call(
        functools.partial(_ret_prompt_kernel, n_heads=n_heads, dk=dk, dv=dv),
        grid=(n_seq, n_chunks),
        in_specs=[pl.BlockSpec((3, chunk, v_w), lambda b, c: (0, b * n_chunks + c, 0)),
                  tab, tab, pl.BlockSpec(memory_space=pl.ANY)],
        out_specs=[pl.BlockSpec((chunk, v_w), lambda b, c: (b * n_chunks + c, 0)),
                   pl.BlockSpec((None, n_heads, dk, dv), lambda b, c: (b, 0, 0, 0))],
        out_shape=[jax.ShapeDtypeStruct((n, v_w), BF16),
                   jax.ShapeDtypeStruct((n_seq, n_heads, dk, dv), F32)],
        scratch_shapes=[pltpu.VMEM((n_heads, dk, dv), F32)],
        input_output_aliases={3: 0},
        compiler_params=_params("parallel", "arbitrary"),
        name="retention_prompt",
    )(p, cos, sin, y_init)


def _ret_decode_kernel(p_ref, cos_ref, sin_ref, s_in_ref, y_in_ref, y_ref, s_out_ref, p_sc, s_sc,
                       *, n_heads, dk, dv, dec_seq):
    del y_in_ref
    p_sc[...] = jnp.zeros_like(p_sc)
    p_sc[:, 0:SAMPLE_PAD, :] = p_ref[...]
    s_sc[...] = s_in_ref[...]

    def store(h, val):
        y_ref[:, h * dv:(h + 1) * dv] = val[0:SAMPLE_PAD].astype(y_ref.dtype)

    _retention_chunk(p_sc, cos_ref[...], sin_ref[...], s_sc, store,
                     n_heads=n_heads, dk=dk, dv=dv, valid=dec_seq)
    s_out_ref[...] = s_sc[...]


def retention_decode(p, cos, sin, state, layer, y_init, n_prompt, n_heads, dk, dv, dec_seq):
    _, n, v_w = p.shape
    n_seq = state.shape[1]
    base = n_prompt // SAMPLE_PAD
    tab = pl.BlockSpec((RET_DECODE_ROWS, dk // 2), lambda b: (0, 0))
    return pl.pallas_call(
        functools.partial(_ret_decode_kernel, n_heads=n_heads, dk=dk, dv=dv, dec_seq=dec_seq),
        grid=(n_seq,),
        in_specs=[pl.BlockSpec((3, SAMPLE_PAD, v_w), lambda b: (0, base + b, 0)), tab, tab,
                  pl.BlockSpec((None, None, n_heads, dk, dv), lambda b: (layer, b, 0, 0, 0)),
                  pl.BlockSpec(memory_space=pl.ANY)],
        out_specs=[pl.BlockSpec((SAMPLE_PAD, v_w), lambda b: (base + b, 0)),
                   pl.BlockSpec((None, n_heads, dk, dv), lambda b: (b, 0, 0, 0))],
        out_shape=[jax.ShapeDtypeStruct((n, v_w), BF16),
                   jax.ShapeDtypeStruct((n_seq, n_heads, dk, dv), F32)],
        scratch_shapes=[pltpu.VMEM((3, RET_DECODE_ROWS, v_w), F32),
                        pltpu.VMEM((n_heads, dk, dv), F32)],
        input_output_aliases={4: 0},
        compiler_params=_params("parallel"),
        name="retention_decode",
    )(p, cos, sin, state, y_init)


def _split_bf16(a):
    hi = a.astype(BF16)
    return hi, (a - hi.astype(F32)).astype(BF16)


def _router_kernel(x_ref, wt_ref, b_ref, idx_ref, gate_ref, rank_ref, cnt_ref, base_sc):
    n_exp = wt_ref.shape[0]
    tr = x_ref.shape[0]

    @pl.when(pl.program_id(0) == 0)
    def _():
        base_sc[...] = jnp.zeros_like(base_sc)

    nt = (((1,), (1,)), ((), ()))
    xh, xl = _split_bf16(x_ref[...])
    wh, wl = _split_bf16(wt_ref[...])
    logits = (lax.dot_general(wh, xh, nt, preferred_element_type=F32)
              + lax.dot_general(wh, xl, nt, preferred_element_type=F32)
              + lax.dot_general(wl, xh, nt, preferred_element_type=F32)) + b_ref[...]
    eidx = lax.broadcasted_iota(jnp.int32, (n_exp, tr), 0)
    r = lax.broadcasted_iota(jnp.int32, (tr, tr), 0)
    c = lax.broadcasted_iota(jnp.int32, (tr, tr), 1)
    before = jnp.where(r < c, 1.0, 0.0).astype(BF16)
    work = logits
    vals, sels = [], []
    seen = base_sc[...]
    for kk in range(TOP_K):
        m = jnp.max(work, axis=0, keepdims=True)
        sel = jnp.min(jnp.where(work == m, eidx, n_exp), axis=0, keepdims=True)
        hit = eidx == sel
        work = jnp.where(hit, -jnp.inf, work)
        onehot = jnp.where(hit, 1.0, 0.0)
        prefix = jnp.dot(onehot.astype(BF16), before, preferred_element_type=F32)
        rank = jnp.sum(onehot * (seen + prefix), axis=0, keepdims=True)
        rank_ref[kk:kk + 1, :] = rank.astype(jnp.int32)
        seen = seen + jnp.sum(onehot, axis=1, keepdims=True)
        vals.append(m)
        sels.append(sel)
    base_sc[...] = seen
    cnt_ref[...] = jnp.broadcast_to(seen, cnt_ref.shape)
    es = [jnp.exp(v - vals[0]) for v in vals]
    tot = es[0] + es[1] + es[2] + es[3]
    for kk in range(TOP_K):
        idx_ref[kk:kk + 1, :] = sels[kk]
        gate_ref[kk:kk + 1, :] = es[kk] / tot


def moe_router(x, w_router, b_router):
    n, d = x.shape
    n_exp = w_router.shape[1]
    tr = _tile(n, ROUTER_TILE_TARGET, LANES)
    tok = pl.BlockSpec((TOP_K, tr), lambda i: (0, i))
    idx, gates, rank, cnt = pl.pallas_call(
        _router_kernel,
        grid=(n // tr,),
        in_specs=[pl.BlockSpec((tr, d), lambda i: (i, 0)),
                  pl.BlockSpec((n_exp, d), lambda i: (0, 0)),
                  pl.BlockSpec((n_exp, 1), lambda i: (0, 0))],
        out_specs=[tok, tok, tok, pl.BlockSpec((n_exp, LANES), lambda i: (0, 0))],
        out_shape=[jax.ShapeDtypeStruct((TOP_K, n), jnp.int32),
                   jax.ShapeDtypeStruct((TOP_K, n), F32),
                   jax.ShapeDtypeStruct((TOP_K, n), jnp.int32),
                   jax.ShapeDtypeStruct((n_exp, LANES), F32)],
        scratch_shapes=[pltpu.VMEM((n_exp, 1), F32)],
        compiler_params=_params("arbitrary"),
        name="moe_router",
    )(x, w_router.T, b_router.reshape(n_exp, 1))
    return idx, gates, rank, cnt[:, 0].astype(jnp.int32)


def _deinterleave_matrix():
    r = lax.broadcasted_iota(jnp.int32, (MXU_DIM, MXU_DIM), 0)
    c = lax.broadcasted_iota(jnp.int32, (MXU_DIM, MXU_DIM), 1)
    src = jnp.where(c < MXU_DIM // 2, 2 * c, 2 * (c - MXU_DIM // 2) + 1)
    return jnp.where(r == src, 1.0, 0.0).astype(BF16)


def _expert_kernel(be_ref, nv_ref, x_ref, wu_ref, bu_ref, wd_ref, bd_ref, o_ref, wu_sc, wd_sc, glu_sc):
    i = pl.program_id(0)
    valid = i < nv_ref[0]
    fresh = (i == 0) | (be_ref[i] != be_ref[jnp.maximum(i - 1, 0)])
    n_chunks = wu_sc.shape[1] // MXU_DIM
    half = MXU_DIM // 2

    @pl.when(valid & fresh)
    def _():
        perm = _deinterleave_matrix()
        for cc in range(n_chunks):
            cols = slice(cc * MXU_DIM, (cc + 1) * MXU_DIM)
            wu_sc[:, cols] = jnp.dot(wu_ref[:, cols].astype(BF16), perm,
                                     preferred_element_type=F32).astype(BF16)
        wd_sc[...] = wd_ref[...].astype(BF16)

    @pl.when(valid)
    def _():
        xb = x_ref[...].astype(BF16)
        for cc in range(n_chunks):
            cols = slice(cc * MXU_DIM, (cc + 1) * MXU_DIM)
            h = jnp.dot(xb, wu_sc[:, cols], preferred_element_type=F32) + bu_ref[:, cols]
            gate = jnp.minimum(h[:, :half], SWIGLU_LIMIT)
            up = jnp.clip(h[:, half:], -SWIGLU_LIMIT, SWIGLU_LIMIT)
            glu = gate * (1.0 / (1.0 + jnp.exp(-SWIGLU_ALPHA * gate)))
            glu_sc[:, cc * half:(cc + 1) * half] = ((up + 1.0) * glu).astype(BF16)
        o_ref[...] = jnp.dot(glu_sc[...], wd_sc[...], preferred_element_type=F32) + bd_ref[...]

    @pl.when(jnp.logical_not(valid))
    def _():
        o_ref[...] = jnp.zeros_like(o_ref)


def moe_experts(xs, block_e, n_valid, w_up, b_up_perm, w_down, b_down, layer):
    n_rows, d = xs.shape
    bm = MOE_BLOCK_ROWS
    n_exp, _, two_f = w_up.shape[1:]
    f = two_f // 2
    assert two_f % MXU_DIM == 0
    return pl.pallas_call(
        _expert_kernel,
        grid_spec=pltpu.PrefetchScalarGridSpec(
            num_scalar_prefetch=2,
            grid=(n_rows // bm,),
            in_specs=[pl.BlockSpec((bm, d), lambda i, be, nv: (i, 0)),
                      pl.BlockSpec((None, None, d, two_f), lambda i, be, nv: (layer, be[i], 0, 0)),
                      pl.BlockSpec((None, None, 1, two_f), lambda i, be, nv: (layer, be[i], 0, 0)),
                      pl.BlockSpec((None, None, f, d), lambda i, be, nv: (layer, be[i], 0, 0)),
                      pl.BlockSpec((None, None, 1, d), lambda i, be, nv: (layer, be[i], 0, 0))],
            out_specs=pl.BlockSpec((bm, d), lambda i, be, nv: (i, 0)),
            scratch_shapes=[pltpu.VMEM((d, two_f), BF16), pltpu.VMEM((f, d), BF16),
                            pltpu.VMEM((bm, f), BF16)]),
        out_shape=jax.ShapeDtypeStruct((n_rows, d), F32),
        compiler_params=_params("arbitrary"),
        name="moe_experts",
    )(block_e, n_valid, xs, w_up, b_up_perm, w_down, b_down)


def _deinterleave_bias(b_up):
    l, e, two_f = b_up.shape
    b = b_up.reshape(l, e, two_f // MXU_DIM, MXU_DIM // 2, 2)
    return jnp.swapaxes(b, -1, -2).reshape(l, e, 1, two_f)


def sc_gather_rows(table, idx):
    n_idx = idx.shape[0]
    _, d = table.shape
    n_workers = SC_CORES * SC_SUBCORES
    assert n_idx % (n_workers * SUBLANES) == 0
    per_worker = n_idx // n_workers
    chunk = _tile(per_worker, SC_GATHER_ROWS, SUBLANES)
    mesh = plsc.VectorSubcoreMesh(core_axis_name="c", subcore_axis_name="s",
                                  num_cores=SC_CORES, num_subcores=SC_SUBCORES)

    n_chunks = per_worker // chunk

    @functools.partial(
        pl.kernel, mesh=mesh,
        out_type=jax.ShapeDtypeStruct((n_idx, d), table.dtype),
        scratch_types=[pltpu.VMEM((chunk,), jnp.int32), pltpu.VMEM((chunk,), jnp.int32),
                       pltpu.VMEM((chunk, d), table.dtype), pltpu.VMEM((chunk, d), table.dtype),
                       pltpu.SemaphoreType.DMA, pltpu.SemaphoreType.DMA])
    def gather(table_hbm, idx_hbm, out_hbm, idx0, idx1, rows0, rows1, sem0, sem1):
        worker = lax.axis_index("s") * SC_CORES + lax.axis_index("c")
        base = worker * per_worker
        bufs = ((idx0, rows0, sem0), (idx1, rows1, sem1))

        def start(c, buf):
            idx_v, rows_v, sem = buf
            off = pl.multiple_of(base + c * chunk, SUBLANES)
            pltpu.sync_copy(idx_hbm.at[pl.ds(off, chunk)], idx_v)
            pltpu.async_copy(table_hbm.at[idx_v], rows_v, sem)

        def finish(c, buf):
            idx_v, rows_v, sem = buf
            pltpu.make_async_copy(table_hbm.at[idx_v], rows_v, sem).wait()
            off = pl.multiple_of(base + c * chunk, SUBLANES)
            pltpu.sync_copy(rows_v, out_hbm.at[pl.ds(off, chunk)])

        start(0, bufs[0])

        @pl.loop(0, n_chunks, step=2)
        def _(c):
            for b in range(2):
                @pl.when(c + b + 1 < n_chunks)
                def _():
                    start(c + b + 1, bufs[1 - b])

                @pl.when(c + b < n_chunks)
                def _():
                    finish(c + b, bufs[b])

    return gather(table, idx)


def _combine_ln_kernel(x_ref, y_ref, gate_ref, g_ref, b_ref, o_ref, *, alpha, n_prompt, dec_seq):
    gates = gate_ref[...]
    f = gates[:, 0:1] * y_ref[0]
    for kk in range(1, TOP_K):
        f = f + gates[:, kk:kk + 1] * y_ref[kk]
    y = _layer_norm(alpha * x_ref[...] + f, g_ref[...], b_ref[...])
    keep = _keep_rows(pl.program_id(0), o_ref.shape[0], n_prompt, dec_seq)
    o_ref[...] = jnp.where(keep, y, 0.0)


def combine_ln(x, y4, gates_t, g, b, alpha, n_prompt, dec_seq):
    n, d = x.shape
    tm = _tile(n, ROW_TILE_TARGET // 2, SUBLANES)
    row = pl.BlockSpec((tm, d), lambda i: (i, 0))
    vec = pl.BlockSpec((1, d), lambda i: (0, 0))
    return pl.pallas_call(
        functools.partial(_combine_ln_kernel, alpha=alpha, n_prompt=n_prompt, dec_seq=dec_seq),
        grid=(n // tm,),
        in_specs=[row, pl.BlockSpec((TOP_K, tm, d), lambda i: (0, i, 0)),
                  pl.BlockSpec((tm, TOP_K), lambda i: (i, 0)), vec, vec],
        out_specs=row,
        out_shape=jax.ShapeDtypeStruct((n, d), F32),
        compiler_params=_params("parallel"),
        name="combine_ln",
    )(x, y4, gates_t, g.reshape(1, d), b.reshape(1, d))


def moe_ffn_ln(x, layer, w_router, b_router, w_up, b_up_perm, w_down, b_down, g, b,
               alpha, n_prompt, dec_seq):
    n, d = x.shape
    n_exp = w_router.shape[1]
    bm = MOE_BLOCK_ROWS
    idx, gates, rank, counts = moe_router(x, w_router, b_router)
    n_blocks = -(-(n * TOP_K + n_exp * (bm - 1)) // bm)
    padded = (counts + bm - 1) // bm * bm
    pad_end = jnp.cumsum(padded)
    pad_start = pad_end - padded
    n_valid = pad_end[-1] // bm
    blk_start = jnp.minimum(jnp.arange(n_blocks, dtype=jnp.int32), n_valid - 1) * bm
    block_e = jnp.minimum(jnp.sum(pad_end[None, :] <= blk_start[:, None], axis=1),
                          n_exp - 1).astype(jnp.int32)
    experts = jnp.arange(n_exp, dtype=jnp.int32)[:, None, None]
    dest = rank + jnp.sum(jnp.where(idx[None] == experts, pad_start[:, None, None], 0), axis=0)
    tok = jnp.broadcast_to(jnp.arange(n, dtype=jnp.int32)[None, :], (TOP_K, n))
    rows = jnp.zeros((n_blocks * bm,), jnp.int32).at[dest.reshape(-1)].set(tok.reshape(-1))
    xs = sc_gather_rows(x, rows)
    ys = moe_experts(xs, block_e, n_valid.reshape(1).astype(jnp.int32), w_up, b_up_perm,
                     w_down, b_down.reshape(b_down.shape[0], n_exp, 1, d), layer)
    y4 = sc_gather_rows(ys, dest.reshape(-1)).reshape(TOP_K, n, d)
    return combine_ln(x, y4, gates.T, g, b, alpha, n_prompt, dec_seq)


def kernel(x_prompt, x_sample, cache_k, cache_v, state_ret, page_table, w_sb_qkv, w_sb_out, sb_bias,
           w_ret_in, w_ret_out, ln_mix_g, ln_mix_b, ln_ffn_g, ln_ffn_b, w_router, b_router,
           w_exp_up, b_exp_up, w_exp_down, b_exp_down):
    n_seq, t_len, d = x_prompt.shape
    dec_batch, dec_seq, _ = x_sample.shape
    depth = ln_mix_g.shape[0]
    page, sb_heads, sb_hd = cache_k.shape[2:]
    ret_heads, dk, dv = state_ret.shape[2:]
    past_len = page_table.shape[1] * page
    alpha = (2 * depth) ** 0.25
    n_prompt = n_seq * t_len
    n_sample = dec_batch * SAMPLE_PAD
    assert dec_seq <= SAMPLE_PAD
    sb_scale = sb_hd ** -0.5
    assert math.log2(sb_scale) == round(math.log2(sb_scale))

    xs_pad = jnp.pad(x_sample, ((0, 0), (0, SAMPLE_PAD - dec_seq), (0, 0)))
    x = jnp.concatenate([x_prompt.reshape(n_prompt, d), xs_pad.reshape(n_sample, d)], axis=0)
    n = n_prompt + n_sample

    half = dk // 2
    inv = ROPE_BASE ** (-jnp.arange(half, dtype=F32) / half)
    ang_p = jnp.arange(t_len, dtype=F32)[:, None] * inv[None, :]
    ang_s = (past_len + jnp.arange(RET_DECODE_ROWS, dtype=F32))[:, None] * inv[None, :]
    b_up_perm = _deinterleave_bias(b_exp_up)

    new_k, new_v, new_s = [], [], []
    for i in range(depth):
        j = i // 2
        if i % 2 == 0:
            qkv = stacked_proj(x, w_sb_qkv[j].astype(BF16), 3, sb_scale)
            qkv_s = qkv[:, n_prompt:].reshape(3, n_sample, sb_heads, sb_hd)
            o_s = sb_decode_attention(qkv_s, sb_bias[j], cache_k, cache_v, page_table, j, dec_seq)
            o = jnp.zeros((n, d), BF16).at[n_prompt:].set(o_s.reshape(n_sample, d).astype(BF16))
            o = sb_prompt_attention(qkv, sb_bias[j], o, n_seq, t_len, sb_hd)
            w_out = w_sb_out[j]
            new_k.append(qkv[1])
            new_v.append(qkv[2])
        else:
            p = stacked_proj(x, w_ret_in[j].astype(BF16), 3)
            o = jnp.zeros((n, ret_heads * dv), BF16)
            o, s_s = retention_decode(p, jnp.cos(ang_s), jnp.sin(ang_s), state_ret, j, o,
                                      n_prompt, ret_heads, dk, dv, dec_seq)
            o, s_p = retention_prompt(p, jnp.cos(ang_p), jnp.sin(ang_p), o, n_seq, t_len,
                                      ret_heads, dk, dv)
            w_out = w_ret_out[j]
            new_s.append((s_p, s_s))
        x = out_proj_ln(o, w_out.astype(BF16), x, ln_mix_g[i], ln_mix_b[i], alpha, n_prompt, dec_seq)
        x = moe_ffn_ln(x, i, w_router[i], b_router[i], w_exp_up, b_up_perm, w_exp_down, b_exp_down,
                       ln_ffn_g[i], ln_ffn_b[i], alpha, n_prompt, dec_seq)

    def prompt_rows(a):
        return a[:n_prompt].reshape(n_seq, t_len, sb_heads, sb_hd)

    def sample_rows(a):
        return a[n_prompt:].reshape(dec_batch, SAMPLE_PAD, -1)[:, :dec_seq]

    y_prompt = x[:n_prompt].reshape(n_seq, t_len, d)
    y_sample = sample_rows(x)
    k_prompt = jnp.stack([prompt_rows(a) for a in new_k])
    v_prompt = jnp.stack([prompt_rows(a) for a in new_v])
    k_sample = jnp.stack([sample_rows(a).reshape(dec_batch, dec_seq, sb_heads, sb_hd) for a in new_k])
    v_sample = jnp.stack([sample_rows(a).reshape(dec_batch, dec_seq, sb_heads, sb_hd) for a in new_v])
    state_prompt = jnp.stack([s[0] for s in new_s])
    state_sample = jnp.stack([s[1] for s in new_s])
    return (y_prompt, y_sample, k_prompt, v_prompt, state_prompt, k_sample, v_sample, state_sample)
```

```python
import functools
import math

import jax
import jax.numpy as jnp
from jax import lax
from jax.experimental import pallas as pl
from jax.experimental.pallas import tpu as pltpu
from jax.experimental.pallas import tpu_sc as plsc

F32 = jnp.float32
BF16 = jnp.bfloat16

TOP_K = 4
ROPE_BASE = 10000.0
LN_EPS = 1e-5
SWIGLU_LIMIT = 7.0
SWIGLU_ALPHA = 1.702

LANES = 128
SUBLANES = 8
MXU_DIM = 256
SAMPLE_PAD = SUBLANES
VMEM_LIMIT_BYTES = 56 * 1024 * 1024
ROW_TILE_TARGET = 1024
MOE_BLOCK_ROWS = 256
SB_BLOCK = LANES
SB_QUERY_TILE = 256
SB_KEY_TILE = 512
RET_CHUNK = 256
RET_DECODE_ROWS = 128
ROUTER_TILE_TARGET = 640
DECODE_PAGES_PER_STEP = 8
DECODE_QUERY_ROWS = 16
SC_CORES = 2
SC_SUBCORES = 16
SC_GATHER_ROWS = 32


def _params(*sem):
    return pltpu.CompilerParams(dimension_semantics=sem, vmem_limit_bytes=VMEM_LIMIT_BYTES)


def _tile(n, target, unit):
    best = None
    for t in range(unit, min(n, target) + 1, unit):
        if n % t == 0:
            best = t
    assert best is not None, (n, target, unit)
    return best


def _layer_norm(y, g, b):
    mu = jnp.mean(y, axis=-1, keepdims=True)
    yc = y - mu
    var = jnp.mean(yc * yc, axis=-1, keepdims=True)
    return yc * lax.rsqrt(var + LN_EPS) * g + b


def _keep_rows(tile_idx, tm, n_prompt, dec_seq):
    row = tile_idx * tm + lax.broadcasted_iota(jnp.int32, (tm, 1), 0)
    pad = (row >= n_prompt) & (((row - n_prompt) & (SAMPLE_PAD - 1)) >= dec_seq)
    return jnp.logical_not(pad)


def _stacked_proj_kernel(x_ref, w_ref, o_ref, *, first_scale):
    y = jnp.dot(x_ref[...].astype(BF16), w_ref[...], preferred_element_type=F32)
    if first_scale != 1.0:
        y = y * jnp.where(pl.program_id(1) == 0, first_scale, 1.0)
    o_ref[...] = y


def stacked_proj(x, w_bf16, n_groups, first_scale=1.0):
    n, k = x.shape
    width = w_bf16.shape[1] // n_groups
    tm = _tile(n, ROW_TILE_TARGET, SUBLANES)
    return pl.pallas_call(
        functools.partial(_stacked_proj_kernel, first_scale=first_scale),
        grid=(n // tm, n_groups),
        in_specs=[pl.BlockSpec((tm, k), lambda i, j: (i, 0)),
                  pl.BlockSpec((k, width), lambda i, j: (0, j))],
        out_specs=pl.BlockSpec((None, tm, width), lambda i, j: (j, i, 0)),
        out_shape=jax.ShapeDtypeStruct((n_groups, n, width), F32),
        compiler_params=_params("parallel", "arbitrary"),
        name="stacked_proj",
    )(x, w_bf16)


def _out_proj_ln_kernel(a_ref, w_ref, x_ref, g_ref, b_ref, o_ref, *, alpha, n_prompt, dec_seq):
    h = jnp.dot(a_ref[...], w_ref[...], preferred_element_type=F32)
    y = _layer_norm(alpha * x_ref[...] + h, g_ref[...], b_ref[...])
    keep = _keep_rows(pl.program_id(0), o_ref.shape[0], n_prompt, dec_seq)
    o_ref[...] = jnp.where(keep, y, 0.0)


def out_proj_ln(a, w_bf16, x, g, b, alpha, n_prompt, dec_seq):
    n, k = a.shape
    d = w_bf16.shape[1]
    tm = _tile(n, ROW_TILE_TARGET, SUBLANES)
    return pl.pallas_call(
        functools.partial(_out_proj_ln_kernel, alpha=alpha, n_prompt=n_prompt, dec_seq=dec_seq),
        grid=(n // tm,),
        in_specs=[pl.BlockSpec((tm, k), lambda i: (i, 0)),
                  pl.BlockSpec((k, d), lambda i: (0, 0)),
                  pl.BlockSpec((tm, d), lambda i: (i, 0)),
                  pl.BlockSpec((1, d), lambda i: (0, 0)),
                  pl.BlockSpec((1, d), lambda i: (0, 0))],
        out_specs=pl.BlockSpec((tm, d), lambda i: (i, 0)),
        out_shape=jax.ShapeDtypeStruct((n, d), F32),
        compiler_params=_params("parallel"),
        name="out_proj_ln",
    )(a, w_bf16, x, g.reshape(1, d), b.reshape(1, d))


def _softplus(z):
    sign = jnp.uint32(0x80000000)
    neg_abs = lax.bitcast_convert_type(lax.bitcast_convert_type(z, jnp.uint32) | sign, F32)
    return jnp.maximum(z, 0.0) + jnp.log(1.0 + jnp.exp(neg_abs))


def _suffix_sum(sp, tri):
    hi = sp.astype(BF16)
    lo = (sp - hi.astype(F32)).astype(BF16)
    return jnp.dot(jnp.concatenate([hi, lo], axis=1), tri, preferred_element_type=F32)


def _tri_incl(n):
    r = lax.broadcasted_iota(jnp.int32, (2 * n, n), 0)
    c = lax.broadcasted_iota(jnp.int32, (2 * n, n), 1)
    return jnp.where(jnp.where(r < n, r, r - n) >= c, 1.0, 0.0).astype(BF16)


def _sb_tile(q2, kbs, vbs, bias_col, carry, tri, masks):
    nt = (((1,), (1,)), ((), ()))
    zs, incls = [], []
    for kb, mask in zip(kbs, masks):
        z = lax.dot_general(q2, kb, nt, preferred_element_type=F32) + bias_col
        sp = _softplus(z)
        if mask is not None:
            sp = jnp.where(mask, sp, 0.0)
        zs.append(z)
        incls.append(_suffix_sum(sp, tri))
    pv = None
    for u in reversed(range(len(kbs))):
        a = jnp.exp(zs[u] - incls[u] - carry)
        if masks[u] is not None:
            a = jnp.where(masks[u], a, 0.0)
        part = jnp.dot(a.astype(BF16), vbs[u], preferred_element_type=F32)
        pv = part if pv is None else pv + part
        carry = carry + incls[u][:, 0:1]
    return pv, carry


def _sb_prompt_kernel(bias_ref, q_ref, k_ref, v_ref, o_in_ref, o_ref, *, head_dim, tq, tk):
    del o_in_ref
    pair = pl.program_id(1)
    n_sub = tk // SB_BLOCK
    first = lax.broadcasted_iota(jnp.int32, (tq, LANES), 1) < head_dim
    rows2 = lax.broadcasted_iota(jnp.int32, (2 * tq, 1), 0)
    bias_col = jnp.where(rows2 < tq, bias_ref[2 * pair], bias_ref[2 * pair + 1])
    tri = _tri_incl(SB_BLOCK)
    r2 = lax.broadcasted_iota(jnp.int32, (2 * tq, SB_BLOCK), 0)
    c_minus_r = lax.broadcasted_iota(jnp.int32, (2 * tq, SB_BLOCK), 1) - jnp.where(r2 < tq, r2, r2 - tq)

    def load(ref, k0, n_blocks=n_sub):
        return [ref[pl.ds(pl.multiple_of(k0 + u * SB_BLOCK, SB_BLOCK), SB_BLOCK), :].astype(BF16)
                for u in range(n_blocks)]

    def q_block(i, _):
        q0 = pl.multiple_of(i * tq, tq)
        q = q_ref[pl.ds(q0, tq), :]
        q2 = jnp.concatenate([jnp.where(first, q, 0.0), jnp.where(first, 0.0, q)], axis=0).astype(BF16)
        n_full = q0 // tk
        k0 = pl.multiple_of(n_full * tk, tk)

        def partial_tile(n_blocks):
            def run():
                masks = [c_minus_r < (q0 - k0 - u * SB_BLOCK) for u in range(n_blocks)]
                return _sb_tile(q2, load(k_ref, k0, n_blocks), load(v_ref, k0, n_blocks), bias_col,
                                jnp.zeros((2 * tq, 1), F32), tri, masks)
            return run

        if tq < tk:
            acc, carry = lax.cond(q0 == k0, partial_tile(tq // SB_BLOCK), partial_tile(n_sub))
        else:
            acc, carry = partial_tile(n_sub)()

        def k_tile(jj, state):
            acc, carry = state
            k0 = pl.multiple_of((n_full - 1 - jj) * tk, tk)
            pv, carry = _sb_tile(q2, load(k_ref, k0), load(v_ref, k0), bias_col, carry, tri,
                                 [None] * n_sub)
            return acc + pv, carry

        acc, _ = lax.fori_loop(0, n_full, k_tile, (acc, carry))
        o_ref[pl.ds(q0, tq), :] = jnp.where(first, acc[:tq], acc[tq:]).astype(o_ref.dtype)
        return 0

    lax.fori_loop(0, q_ref.shape[0] // tq, q_block, 0)


def sb_prompt_attention(qkv, bias, o_init, n_seq, t_len, head_dim):
    _, n, d = qkv.shape
    assert 2 * head_dim == LANES
    tk = min(SB_KEY_TILE, t_len)
    tq = min(SB_QUERY_TILE, tk)
    assert t_len % tk == 0 and tk % tq == 0 and tk % SB_BLOCK == 0
    part = lambda g: pl.BlockSpec((None, t_len, LANES), lambda b, p: (g, b, p))
    return pl.pallas_call(
        functools.partial(_sb_prompt_kernel, head_dim=head_dim, tq=tq, tk=tk),
        grid=(n_seq, d // LANES),
        in_specs=[pl.BlockSpec(memory_space=pltpu.SMEM), part(0), part(1), part(2),
                  pl.BlockSpec(memory_space=pl.ANY)],
        out_specs=pl.BlockSpec((t_len, LANES), lambda b, p: (b, p)),
        out_shape=jax.ShapeDtypeStruct((n, d), BF16),
        input_output_aliases={4: 0},
        compiler_params=_params("parallel", "parallel"),
        name="sb_prompt_attention",
    )(bias, qkv, qkv, qkv, o_init)


def _sb_decode_kernel(pt_ref, bias_ref, q_ref, kn_ref, vn_ref, *rest, pages_per_step, dec_seq):
    del pt_ref
    g = pages_per_step
    k_refs, v_refs = rest[:g], rest[g:2 * g]
    o_ref, acc_sc, carry_sc = rest[2 * g:]
    step = pl.program_id(1)
    n_heads, t_pad, hd = q_ref.shape
    page = k_refs[0].shape[-1]
    n_q = n_heads * t_pad
    tri = _tri_incl(page)
    q3 = q_ref[...].astype(BF16)
    bias_col = bias_ref[...]

    def page_update(kts, vts, masks):
        zs, incls = [], []
        for kt, mask in zip(kts, masks):
            z = jnp.einsum('htd,hds->hts', q3, kt, preferred_element_type=F32).reshape(n_q, page)
            z = z + bias_col
            sp = _softplus(z)
            if mask is not None:
                sp = jnp.where(mask, sp, 0.0)
            zs.append(z)
            incls.append(_suffix_sum(sp, tri))
        carry = carry_sc[...]
        acc = acc_sc[...]
        for u in reversed(range(len(kts))):
            a = jnp.exp(zs[u] - incls[u] - carry)
            if masks[u] is not None:
                a = jnp.where(masks[u], a, 0.0)
            a3 = a.reshape(n_heads, t_pad, page).astype(BF16)
            acc = acc + jnp.einsum('hts,hds->htd', a3, vts[u], preferred_element_type=F32)
            carry = carry + incls[u][:, 0:1]
        carry_sc[...] = carry
        acc_sc[...] = acc

    @pl.when(step == 0)
    def _():
        acc_sc[...] = jnp.zeros_like(acc_sc)
        carry_sc[...] = jnp.zeros_like(carry_sc)
        t_query = lax.broadcasted_iota(jnp.int32, (n_q, page), 0) & (t_pad - 1)
        s_key = lax.broadcasted_iota(jnp.int32, (n_q, page), 1)
        page_update([kn_ref[...].astype(BF16)], [vn_ref[...].astype(BF16)],
                    [(s_key < t_query) & (s_key < dec_seq)])

    page_update([kr[...].astype(BF16) for kr in reversed(k_refs)],
                [vr[...].astype(BF16) for vr in reversed(v_refs)], [None] * g)

    @pl.when(step == pl.num_programs(1) - 1)
    def _():
        o_ref[...] = acc_sc[...]


def sb_decode_attention(qkv_s, bias, cache_kt, cache_vt, page_table, layer, dec_seq):
    _, n_seq, t_new, n_heads, hd = qkv_s.shape
    page = cache_kt.shape[-1]
    n_pages = page_table.shape[1]
    g = _tile(n_pages, DECODE_PAGES_PER_STEP, 1)
    t_pad = DECODE_QUERY_ROWS
    assert t_new <= t_pad <= page and t_pad & (t_pad - 1) == 0
    q3 = jnp.pad(jnp.swapaxes(qkv_s[0], 1, 2), ((0, 0), (0, 0), (0, t_pad - t_new), (0, 0)))
    new_t = lambda a: jnp.pad(jnp.transpose(a, (0, 2, 3, 1)), ((0, 0), (0, 0), (0, 0), (0, page - t_new)))
    head = lambda last: pl.BlockSpec((None, n_heads, last[0], last[1]), lambda b, s, pt: (b, 0, 0, 0))

    def page_spec(j):
        return pl.BlockSpec(
            (None, None, n_heads, hd, page),
            lambda b, s, pt: (layer, pt[b * n_pages + (n_pages - 1 - (s * g + j))], 0, 0, 0))

    kv_specs = [page_spec(j) for j in range(g)]
    n_q = n_heads * t_pad
    out = pl.pallas_call(
        functools.partial(_sb_decode_kernel, pages_per_step=g, dec_seq=dec_seq),
        grid_spec=pltpu.PrefetchScalarGridSpec(
            num_scalar_prefetch=1,
            grid=(n_seq, n_pages // g),
            in_specs=[pl.BlockSpec((n_q, 1), lambda b, s, pt: (0, 0)), head((t_pad, hd)),
                      head((hd, page)), head((hd, page))] + kv_specs + kv_specs,
            out_specs=head((t_pad, hd)),
            scratch_shapes=[pltpu.VMEM((n_heads, t_pad, hd), F32), pltpu.VMEM((n_q, 1), F32)]),
        out_shape=jax.ShapeDtypeStruct((n_seq, n_heads, t_pad, hd), F32),
        compiler_params=_params("parallel", "arbitrary"),
        name="sb_decode_attention",
    )(page_table.reshape(-1), jnp.repeat(bias, t_pad).reshape(n_q, 1), q3, new_t(qkv_s[1]),
      new_t(qkv_s[2]), *([cache_kt] * g), *([cache_vt] * g))
    return jnp.swapaxes(out[:, :, :t_new], 1, 2).reshape(n_seq, t_new, n_heads * hd)


def _rope(a, cos, sin):
    half = a.shape[-1] // 2
    a1, a2 = a[:, :half], a[:, half:]
    return jnp.concatenate([a1 * cos - a2 * sin, a1 * sin + a2 * cos], axis=-1)


def _retention_chunk(p_ref, cos, sin, s_sc, store, *, n_heads, dk, dv, valid):
    c_len = p_ref.shape[1]
    qk_w = n_heads * dk
    idx = lax.broadcasted_iota(jnp.int32, (c_len, 1), 0).astype(F32)
    rel = (lax.broadcasted_iota(jnp.int32, (c_len, c_len), 0)
           - lax.broadcasted_iota(jnp.int32, (c_len, c_len), 1)).astype(F32)
    live = lax.broadcasted_iota(jnp.int32, (c_len, 1), 0) < valid
    nt = (((1,), (1,)), ((), ()))
    tn = (((0,), (0,)), ((), ()))
    for h in range(n_heads):
        log_g = math.log1p(-2.0 ** (-5.0 - h))
        q = _rope(p_ref[0, :, h * dk:(h + 1) * dk], cos, sin)
        k = _rope(p_ref[0, :, qk_w + h * dk:qk_w + (h + 1) * dk], cos, sin) * (dk ** -0.5)
        k = jnp.where(live, k, 0.0)
        v = p_ref[1, :, h * dv:(h + 1) * dv].astype(BF16)
        gate = p_ref[2, :, h * dv:(h + 1) * dv]
        intra = jnp.where(rel >= 0.0, jnp.exp(log_g * jnp.maximum(rel, 0.0)), 0.0)
        q_dec = jnp.exp(log_g * (idx + 1.0))
        k_dec = jnp.exp(log_g * (valid - 1.0 - idx))
        qb = q.astype(BF16)
        sc = lax.dot_general(qb, k.astype(BF16), nt, preferred_element_type=F32) * intra
        s_old = s_sc[h]
        o = (jnp.dot(sc.astype(BF16), v, preferred_element_type=F32)
             + jnp.dot(qb, s_old.astype(BF16), preferred_element_type=F32) * q_dec)
        s_sc[h] = (math.exp(log_g * valid) * s_old
                   + lax.dot_general((k * k_dec).astype(BF16), v, tn, preferred_element_type=F32))
        mu = jnp.mean(o, axis=-1, keepdims=True)
        oc = o - mu
        var = jnp.mean(oc * oc, axis=-1, keepdims=True)
        on = oc * lax.rsqrt(var + LN_EPS)
        store(h, gate * (1.0 / (1.0 + jnp.exp(-gate))) * on)


def _ret_prompt_kernel(p_ref, cos_ref, sin_ref, y_in_ref, y_ref, s_out_ref, s_sc, *, n_heads, dk, dv):
    del y_in_ref
    c = pl.program_id(1)

    @pl.when(c == 0)
    def _():
        s_sc[...] = jnp.zeros_like(s_sc)

    def store(h, val):
        y_ref[:, h * dv:(h + 1) * dv] = val.astype(y_ref.dtype)

    _retention_chunk(p_ref, cos_ref[...], sin_ref[...], s_sc, store,
                     n_heads=n_heads, dk=dk, dv=dv, valid=p_ref.shape[1])

    @pl.when(c == pl.num_programs(1) - 1)
    def _():
        s_out_ref[...] = s_sc[...]


def retention_prompt(p, cos, sin, y_init, n_seq, t_len, n_heads, dk, dv):
    _, n, v_w = p.shape
    assert v_w == n_heads * dv == 2 * n_heads * dk
    chunk = _tile(t_len, RET_CHUNK, SUBLANES)
    n_chunks = t_len // chunk
    tab = pl.BlockSpec((chunk, dk // 2), lambda b, c: (c, 0))
    return pl.pallas_call(
        functools.partial(_ret_prompt_kernel, n_heads=n_heads, dk=dk, dv=dv),
        grid=(n_seq, n_chunks),
        in_specs=[pl.BlockSpec((3, chunk, v_w), lambda b, c: (0, b * n_chunks + c, 0)),
                  tab, tab, pl.BlockSpec(memory_space=pl.ANY)],
        out_specs=[pl.BlockSpec((chunk, v_w), lambda b, c: (b * n_chunks + c, 0)),
                   pl.BlockSpec((None, n_heads, dk, dv), lambda b, c: (b, 0, 0, 0))],
        out_shape=[jax.ShapeDtypeStruct((n, v_w), BF16),
                   jax.ShapeDtypeStruct((n_seq, n_heads, dk, dv), F32)],
        scratch_shapes=[pltpu.VMEM((n_heads, dk, dv), F32)],
        input_output_aliases={3: 0},
        compiler_params=_params("parallel", "arbitrary"),
        name="retention_prompt",
    )(p, cos, sin, y_init)


def _ret_decode_kernel(p_ref, cos_ref, sin_ref, s_in_ref, y_in_ref, y_ref, s_out_ref, p_sc, s_sc,
                       *, n_heads, dk, dv, dec_seq):
    del y_in_ref
    p_sc[...] = jnp.zeros_like(p_sc)
    p_sc[:, 0:SAMPLE_PAD, :] = p_ref[...]
    s_sc[...] = s_in_ref[...]

    def store(h, val):
        y_ref[:, h * dv:(h + 1) * dv] = val[0:SAMPLE_PAD].astype(y_ref.dtype)

    _retention_chunk(p_sc, cos_ref[...], sin_ref[...], s_sc, store,
                     n_heads=n_heads, dk=dk, dv=dv, valid=dec_seq)
    s_out_ref[...] = s_sc[...]


def retention_decode(p, cos, sin, state, layer, y_init, n_prompt, n_heads, dk, dv, dec_seq):
    _, n, v_w = p.shape
    n_seq = state.shape[1]
    base = n_prompt // SAMPLE_PAD
    tab = pl.BlockSpec((RET_DECODE_ROWS, dk // 2), lambda b: (0, 0))
    return pl.pallas_call(
        functools.partial(_ret_decode_kernel, n_heads=n_heads, dk=dk, dv=dv, dec_seq=dec_seq),
        grid=(n_seq,),
        in_specs=[pl.BlockSpec((3, SAMPLE_PAD, v_w), lambda b: (0, base + b, 0)), tab, tab,
                  pl.BlockSpec((None, None, n_heads, dk, dv), lambda b: (layer, b, 0, 0, 0)),
                  pl.BlockSpec(memory_space=pl.ANY)],
        out_specs=[pl.BlockSpec((SAMPLE_PAD, v_w), lambda b: (base + b, 0)),
                   pl.BlockSpec((None, n_heads, dk, dv), lambda b: (b, 0, 0, 0))],
        out_shape=[jax.ShapeDtypeStruct((n, v_w), BF16),
                   jax.ShapeDtypeStruct((n_seq, n_heads, dk, dv), F32)],
        scratch_shapes=[pltpu.VMEM((3, RET_DECODE_ROWS, v_w), F32),
                        pltpu.VMEM((n_heads, dk, dv), F32)],
        input_output_aliases={4: 0},
        compiler_params=_params("parallel"),
        name="retention_decode",
    )(p, cos, sin, state, y_init)


def _split_bf16(a):
    hi = a.astype(BF16)
    return hi, (a - hi.astype(F32)).astype(BF16)


def _router_kernel(x_ref, wt_ref, b_ref, idx_ref, gate_ref, rank_ref, cnt_ref, base_sc):
    n_exp = wt_ref.shape[0]
    tr = x_ref.shape[0]

    @pl.when(pl.program_id(0) == 0)
    def _():
        base_sc[...] = jnp.zeros_like(base_sc)

    nt = (((1,), (1,)), ((), ()))
    xh, xl = _split_bf16(x_ref[...])
    wh, wl = _split_bf16(wt_ref[...])
    logits = (lax.dot_general(wh, xh, nt, preferred_element_type=F32)
              + lax.dot_general(wh, xl, nt, preferred_element_type=F32)
              + lax.dot_general(wl, xh, nt, preferred_element_type=F32)) + b_ref[...]
    eidx = lax.broadcasted_iota(jnp.int32, (n_exp, tr), 0)
    r = lax.broadcasted_iota(jnp.int32, (tr, tr), 0)
    c = lax.broadcasted_iota(jnp.int32, (tr, tr), 1)
    before = jnp.where(r < c, 1.0, 0.0).astype(BF16)
    work = logits
    vals, sels = [], []
    seen = base_sc[...]
    for kk in range(TOP_K):
        m = jnp.max(work, axis=0, keepdims=True)
        sel = jnp.min(jnp.where(work == m, eidx, n_exp), axis=0, keepdims=True)
        hit = eidx == sel
        work = jnp.where(hit, -jnp.inf, work)
        onehot = jnp.where(hit, 1.0, 0.0)
        prefix = jnp.dot(onehot.astype(BF16), before, preferred_element_type=F32)
        rank = jnp.sum(onehot * (seen + prefix), axis=0, keepdims=True)
        rank_ref[kk:kk + 1, :] = rank.astype(jnp.int32)
        seen = seen + jnp.sum(onehot, axis=1, keepdims=True)
        vals.append(m)
        sels.append(sel)
    base_sc[...] = seen
    cnt_ref[...] = jnp.broadcast_to(seen, cnt_ref.shape)
    es = [jnp.exp(v - vals[0]) for v in vals]
    tot = es[0] + es[1] + es[2] + es[3]
    for kk in range(TOP_K):
        idx_ref[kk:kk + 1, :] = sels[kk]
        gate_ref[kk:kk + 1, :] = es[kk] / tot


def moe_router(x, w_router, b_router):
    n, d = x.shape
    n_exp = w_router.shape[1]
    tr = _tile(n, ROUTER_TILE_TARGET, LANES)
    tok = pl.BlockSpec((TOP_K, tr), lambda i: (0, i))
    idx, gates, rank, cnt = pl.pallas_call(
        _router_kernel,
        grid=(n // tr,),
        in_specs=[pl.BlockSpec((tr, d), lambda i: (i, 0)),
                  pl.BlockSpec((n_exp, d), lambda i: (0, 0)),
                  pl.BlockSpec((n_exp, 1), lambda i: (0, 0))],
        out_specs=[tok, tok, tok, pl.BlockSpec((n_exp, LANES), lambda i: (0, 0))],
        out_shape=[jax.ShapeDtypeStruct((TOP_K, n), jnp.int32),
                   jax.ShapeDtypeStruct((TOP_K, n), F32),
                   jax.ShapeDtypeStruct((TOP_K, n), jnp.int32),
                   jax.ShapeDtypeStruct((n_exp, LANES), F32)],
        scratch_shapes=[pltpu.VMEM((n_exp, 1), F32)],
        compiler_params=_params("arbitrary"),
        name="moe_router",
    )(x, w_router.T, b_router.reshape(n_exp, 1))
    return idx, gates, rank, cnt[:, 0].astype(jnp.int32)


def _deinterleave_matrix():
    r = lax.broadcasted_iota(jnp.int32, (MXU_DIM, MXU_DIM), 0)
    c = lax.broadcasted_iota(jnp.int32, (MXU_DIM, MXU_DIM), 1)
    src = jnp.where(c < MXU_DIM // 2, 2 * c, 2 * (c - MXU_DIM // 2) + 1)
    return jnp.where(r == src, 1.0, 0.0).astype(BF16)


def _expert_kernel(be_ref, nv_ref, x_ref, wu_ref, bu_ref, wd_ref, bd_ref, o_ref, wu_sc, wd_sc, glu_sc):
    i = pl.program_id(0)
    valid = i < nv_ref[0]
    fresh = (i == 0) | (be_ref[i] != be_ref[jnp.maximum(i - 1, 0)])
    n_chunks = wu_sc.shape[1] // MXU_DIM
    half = MXU_DIM // 2

    @pl.when(valid & fresh)
    def _():
        perm = _deinterleave_matrix()
        for cc in range(n_chunks):
            cols = slice(cc * MXU_DIM, (cc + 1) * MXU_DIM)
            wu_sc[:, cols] = jnp.dot(wu_ref[:, cols].astype(BF16), perm,
                                     preferred_element_type=F32).astype(BF16)
        wd_sc[...] = wd_ref[...].astype(BF16)

    @pl.when(valid)
    def _():
        xb = x_ref[...].astype(BF16)
        for cc in range(n_chunks):
            cols = slice(cc * MXU_DIM, (cc + 1) * MXU_DIM)
            h = jnp.dot(xb, wu_sc[:, cols], preferred_element_type=F32) + bu_ref[:, cols]
            gate = jnp.minimum(h[:, :half], SWIGLU_LIMIT)
            up = jnp.clip(h[:, half:], -SWIGLU_LIMIT, SWIGLU_LIMIT)
            glu = gate * (1.0 / (1.0 + jnp.exp(-SWIGLU_ALPHA * gate)))
            glu_sc[:, cc * half:(cc + 1) * half] = ((up + 1.0) * glu).astype(BF16)
        o_ref[...] = jnp.dot(glu_sc[...], wd_sc[...], preferred_element_type=F32) + bd_ref[...]

    @pl.when(jnp.logical_not(valid))
    def _():
        o_ref[...] = jnp.zeros_like(o_ref)


def moe_experts(xs, block_e, n_valid, w_up, b_up_perm, w_down, b_down, layer):
    n_rows, d = xs.shape
    bm = MOE_BLOCK_ROWS
    n_exp, _, two_f = w_up.shape[1:]
    f = two_f // 2
    assert two_f % MXU_DIM == 0
    return pl.pallas_call(
        _expert_kernel,
        grid_spec=pltpu.PrefetchScalarGridSpec(
            num_scalar_prefetch=2,
            grid=(n_rows // bm,),
            in_specs=[pl.BlockSpec((bm, d), lambda i, be, nv: (i, 0)),
                      pl.BlockSpec((None, None, d, two_f), lambda i, be, nv: (layer, be[i], 0, 0)),
                      pl.BlockSpec((None, None, 1, two_f), lambda i, be, nv: (layer, be[i], 0, 0)),
                      pl.BlockSpec((None, None, f, d), lambda i, be, nv: (layer, be[i], 0, 0)),
                      pl.BlockSpec((None, None, 1, d), lambda i, be, nv: (layer, be[i], 0, 0))],
            out_specs=pl.BlockSpec((bm, d), lambda i, be, nv: (i, 0)),
            scratch_shapes=[pltpu.VMEM((d, two_f), BF16), pltpu.VMEM((f, d), BF16),
                            pltpu.VMEM((bm, f), BF16)]),
        out_shape=jax.ShapeDtypeStruct((n_rows, d), F32),
        compiler_params=_params("arbitrary"),
        name="moe_experts",
    )(block_e, n_valid, xs, w_up, b_up_perm, w_down, b_down)


def _deinterleave_bias(b_up):
    l, e, two_f = b_up.shape
    b = b_up.reshape(l, e, two_f // MXU_DIM, MXU_DIM // 2, 2)
    return jnp.swapaxes(b, -1, -2).reshape(l, e, 1, two_f)


def sc_gather_rows(table, idx):
    n_idx = idx.shape[0]
    _, d = table.shape
    n_workers = SC_CORES * SC_SUBCORES
    assert n_idx % (n_workers * SUBLANES) == 0
    per_worker = n_idx // n_workers
    chunk = _tile(per_worker, SC_GATHER_ROWS, SUBLANES)
    mesh = plsc.VectorSubcoreMesh(core_axis_name="c", subcore_axis_name="s",
                                  num_cores=SC_CORES, num_subcores=SC_SUBCORES)

    n_chunks = per_worker // chunk

    @functools.partial(
        pl.kernel, mesh=mesh,
        out_type=jax.ShapeDtypeStruct((n_idx, d), table.dtype),
        scratch_types=[pltpu.VMEM((chunk,), jnp.int32), pltpu.VMEM((chunk,), jnp.int32),
                       pltpu.VMEM((chunk, d), table.dtype), pltpu.VMEM((chunk, d), table.dtype),
                       pltpu.SemaphoreType.DMA, pltpu.SemaphoreType.DMA])
    def gather(table_hbm, idx_hbm, out_hbm, idx0, idx1, rows0, rows1, sem0, sem1):
        worker = lax.axis_index("s") * SC_CORES + lax.axis_index("c")
        base = worker * per_worker
        bufs = ((idx0, rows0, sem0), (idx1, rows1, sem1))

        def start(c, buf):
            idx_v, rows_v, sem = buf
            off = pl.multiple_of(base + c * chunk, SUBLANES)
            pltpu.sync_copy(idx_hbm.at[pl.ds(off, chunk)], idx_v)
            pltpu.async_copy(table_hbm.at[idx_v], rows_v, sem)

        def finish(c, buf):
            idx_v, rows_v, sem = buf
            pltpu.make_async_copy(table_hbm.at[idx_v], rows_v, sem).wait()
            off = pl.multiple_of(base + c * chunk, SUBLANES)
            pltpu.sync_copy(rows_v, out_hbm.at[pl.ds(off, chunk)])

        start(0, bufs[0])

        @pl.loop(0, n_chunks, step=2)
        def _(c):
            for b in range(2):
                @pl.when(c + b + 1 < n_chunks)
                def _():
                    start(c + b + 1, bufs[1 - b])

                @pl.when(c + b < n_chunks)
                def _():
                    finish(c + b, bufs[b])

    return gather(table, idx)


def _combine_ln_kernel(x_ref, y_ref, gate_ref, g_ref, b_ref, o_ref, *, alpha, n_prompt, dec_seq):
    gates = gate_ref[...]
    f = gates[:, 0:1] * y_ref[0]
    for kk in range(1, TOP_K):
        f = f + gates[:, kk:kk + 1] * y_ref[kk]
    y = _layer_norm(alpha * x_ref[...] + f, g_ref[...], b_ref[...])
    keep = _keep_rows(pl.program_id(0), o_ref.shape[0], n_prompt, dec_seq)
    o_ref[...] = jnp.where(keep, y, 0.0)


def combine_ln(x, y4, gates_t, g, b, alpha, n_prompt, dec_seq):
    n, d = x.shape
    tm = _tile(n, ROW_TILE_TARGET // 2, SUBLANES)
    row = pl.BlockSpec((tm, d), lambda i: (i, 0))
    vec = pl.BlockSpec((1, d), lambda i: (0, 0))
    return pl.pallas_call(
        functools.partial(_combine_ln_kernel, alpha=alpha, n_prompt=n_prompt, dec_seq=dec_seq),
        grid=(n // tm,),
        in_specs=[row, pl.BlockSpec((TOP_K, tm, d), lambda i: (0, i, 0)),
                  pl.BlockSpec((tm, TOP_K), lambda i: (i, 0)), vec, vec],
        out_specs=row,
        out_shape=jax.ShapeDtypeStruct((n, d), F32),
        compiler_params=_params("parallel"),
        name="combine_ln",
    )(x, y4, gates_t, g.reshape(1, d), b.reshape(1, d))


def moe_ffn_ln(x, layer, w_router, b_router, w_up, b_up_perm, w_down, b_down, g, b,
               alpha, n_prompt, dec_seq):
    n, d = x.shape
    n_exp = w_router.shape[1]
    bm = MOE_BLOCK_ROWS
    idx, gates, rank, counts = moe_router(x, w_router, b_router)
    n_blocks = -(-(n * TOP_K + n_exp * (bm - 1)) // bm)
    padded = (counts + bm - 1) // bm * bm
    pad_end = jnp.cumsum(padded)
    pad_start = pad_end - padded
    n_valid = pad_end[-1] // bm
    blk_start = jnp.minimum(jnp.arange(n_blocks, dtype=jnp.int32), n_valid - 1) * bm
    block_e = jnp.minimum(jnp.sum(pad_end[None, :] <= blk_start[:, None], axis=1),
                          n_exp - 1).astype(jnp.int32)
    experts = jnp.arange(n_exp, dtype=jnp.int32)[:, None, None]
    dest = rank + jnp.sum(jnp.where(idx[None] == experts, pad_start[:, None, None], 0), axis=0)
    tok = jnp.broadcast_to(jnp.arange(n, dtype=jnp.int32)[None, :], (TOP_K, n))
    rows = jnp.zeros((n_blocks * bm,), jnp.int32).at[dest.reshape(-1)].set(tok.reshape(-1))
    xs = sc_gather_rows(x, rows)
    ys = moe_experts(xs, block_e, n_valid.reshape(1).astype(jnp.int32), w_up, b_up_perm,
                     w_down, b_down.reshape(b_down.shape[0], n_exp, 1, d), layer)
    y4 = sc_gather_rows(ys, dest.reshape(-1)).reshape(TOP_K, n, d)
    return combine_ln(x, y4, gates.T, g, b, alpha, n_prompt, dec_seq)


def kernel(x_prompt, x_sample, cache_k, cache_v, state_ret, page_table, w_sb_qkv, w_sb_out, sb_bias,
           w_ret_in, w_ret_out, ln_mix_g, ln_mix_b, ln_ffn_g, ln_ffn_b, w_router, b_router,
           w_exp_up, b_exp_up, w_exp_down, b_exp_down):
    n_seq, t_len, d = x_prompt.shape
    dec_batch, dec_seq, _ = x_sample.shape
    depth = ln_mix_g.shape[0]
    page, sb_heads, sb_hd = cache_k.shape[2:]
    ret_heads, dk, dv = state_ret.shape[2:]
    past_len = page_table.shape[1] * page
    alpha = (2 * depth) ** 0.25
    n_prompt = n_seq * t_len
    n_sample = dec_batch * SAMPLE_PAD
    assert dec_seq <= SAMPLE_PAD
    sb_scale = sb_hd ** -0.5
    assert math.log2(sb_scale) == round(math.log2(sb_scale))

    xs_pad = jnp.pad(x_sample, ((0, 0), (0, SAMPLE_PAD - dec_seq), (0, 0)))
    x = jnp.concatenate([x_prompt.reshape(n_prompt, d), xs_pad.reshape(n_sample, d)], axis=0)
    n = n_prompt + n_sample

    half = dk // 2
    inv = ROPE_BASE ** (-jnp.arange(half, dtype=F32) / half)
    ang_p = jnp.arange(t_len, dtype=F32)[:, None] * inv[None, :]
    ang_s = (past_len + jnp.arange(RET_DECODE_ROWS, dtype=F32))[:, None] * inv[None, :]
    b_up_perm = _deinterleave_bias(b_exp_up)
    cache_kt = jnp.transpose(cache_k, (0, 1, 3, 4, 2))
    cache_vt = jnp.transpose(cache_v, (0, 1, 3, 4, 2))

    new_k, new_v, new_s = [], [], []
    for i in range(depth):
        j = i // 2
        if i % 2 == 0:
            qkv = stacked_proj(x, w_sb_qkv[j].astype(BF16), 3, sb_scale)
            qkv_s = qkv[:, n_prompt:].reshape(3, dec_batch, SAMPLE_PAD, sb_heads, sb_hd)
            o_s = sb_decode_attention(qkv_s, sb_bias[j], cache_kt, cache_vt, page_table, j, dec_seq)
            o = jnp.zeros((n, d), BF16).at[n_prompt:].set(o_s.reshape(n_sample, d).astype(BF16))
            o = sb_prompt_attention(qkv, sb_bias[j], o, n_seq, t_len, sb_hd)
            w_out = w_sb_out[j]
            new_k.append(qkv[1])
            new_v.append(qkv[2])
        else:
            p = stacked_proj(x, w_ret_in[j].astype(BF16), 3)
            o = jnp.zeros((n, ret_heads * dv), BF16)
            o, s_s = retention_decode(p, jnp.cos(ang_s), jnp.sin(ang_s), state_ret, j, o,
                                      n_prompt, ret_heads, dk, dv, dec_seq)
            o, s_p = retention_prompt(p, jnp.cos(ang_p), jnp.sin(ang_p), o, n_seq, t_len,
                                      ret_heads, dk, dv)
            w_out = w_ret_out[j]
            new_s.append((s_p, s_s))
        x = out_proj_ln(o, w_out.astype(BF16), x, ln_mix_g[i], ln_mix_b[i], alpha, n_prompt, dec_seq)
        x = moe_ffn_ln(x, i, w_router[i], b_router[i], w_exp_up, b_up_perm, w_exp_down, b_exp_down,
                       ln_ffn_g[i], ln_ffn_b[i], alpha, n_prompt, dec_seq)

    def prompt_rows(a):
        return a[:n_prompt].reshape(n_seq, t_len, sb_heads, sb_hd)

    def sample_rows(a):
        return a[n_prompt:].reshape(dec_batch, SAMPLE_PAD, -1)[:, :dec_seq]

    y_prompt = x[:n_prompt].reshape(n_seq, t_len, d)
    y_sample = sample_rows(x)
    k_prompt = jnp.stack([prompt_rows(a) for a in new_k])
    v_prompt = jnp.stack([prompt_rows(a) for a in new_v])
    k_sample = jnp.stack([sample_rows(a).reshape(dec_batch, dec_seq, sb_heads, sb_hd) for a in new_k])
    v_sample = jnp.stack([sample_rows(a).reshape(dec_batch, dec_seq, sb_heads, sb_hd) for a in new_v])
    state_prompt = jnp.stack([s[0] for s in new_s])
    state_sample = jnp.stack([s[1] for s in new_s])
    return (y_prompt, y_sample, k_prompt, v_prompt, state_prompt, k_sample, v_sample, state_sample)
```

```python
import functools
import math

import jax
import jax.numpy as jnp
from jax import lax
from jax.experimental import pallas as pl
from jax.experimental.pallas import tpu as pltpu
from jax.experimental.pallas import tpu_sc as plsc

F32 = jnp.float32
BF16 = jnp.bfloat16

TOP_K = 4
ROPE_BASE = 10000.0
LN_EPS = 1e-5
SWIGLU_LIMIT = 7.0
SWIGLU_ALPHA = 1.702

LANES = 128
SUBLANES = 8
MXU_DIM = 256
SAMPLE_PAD = SUBLANES
VMEM_LIMIT_BYTES = 56 * 1024 * 1024
ROW_TILE_TARGET = 1024
MOE_BLOCK_ROWS = 256
SB_BLOCK = LANES
SB_QUERY_TILE = 512
SB_KEY_TILE = 512
RET_CHUNK = 256
RET_DECODE_ROWS = 128
ROUTER_TILE_TARGET = 640
DECODE_PAGES_PER_STEP = 8
DECODE_QUERY_ROWS = 16
SC_CORES = 2
SC_SUBCORES = 16
SC_GATHER_ROWS = 32


def _params(*sem):
    return pltpu.CompilerParams(dimension_semantics=sem, vmem_limit_bytes=VMEM_LIMIT_BYTES)


def _tile(n, target, unit):
    best = None
    for t in range(unit, min(n, target) + 1, unit):
        if n % t == 0:
            best = t
    assert best is not None, (n, target, unit)
    return best


def _layer_norm(y, g, b):
    mu = jnp.mean(y, axis=-1, keepdims=True)
    yc = y - mu
    var = jnp.mean(yc * yc, axis=-1, keepdims=True)
    return yc * lax.rsqrt(var + LN_EPS) * g + b


def _keep_rows(tile_idx, tm, n_prompt, dec_seq):
    row = tile_idx * tm + lax.broadcasted_iota(jnp.int32, (tm, 1), 0)
    pad = (row >= n_prompt) & (((row - n_prompt) & (SAMPLE_PAD - 1)) >= dec_seq)
    return jnp.logical_not(pad)


def _stacked_proj_kernel(x_ref, w_ref, o_ref, *, first_scale):
    y = jnp.dot(x_ref[...].astype(BF16), w_ref[...], preferred_element_type=F32)
    if first_scale != 1.0:
        y = y * jnp.where(pl.program_id(1) == 0, first_scale, 1.0)
    o_ref[...] = y


def stacked_proj(x, w_bf16, n_groups, first_scale=1.0):
    n, k = x.shape
    width = w_bf16.shape[1] // n_groups
    tm = _tile(n, ROW_TILE_TARGET, SUBLANES)
    return pl.pallas_call(
        functools.partial(_stacked_proj_kernel, first_scale=first_scale),
        grid=(n // tm, n_groups),
        in_specs=[pl.BlockSpec((tm, k), lambda i, j: (i, 0)),
                  pl.BlockSpec((k, width), lambda i, j: (0, j))],
        out_specs=pl.BlockSpec((None, tm, width), lambda i, j: (j, i, 0)),
        out_shape=jax.ShapeDtypeStruct((n_groups, n, width), F32),
        compiler_params=_params("parallel", "arbitrary"),
        name="stacked_proj",
    )(x, w_bf16)


def _out_proj_ln_kernel(a_ref, w_ref, x_ref, g_ref, b_ref, o_ref, *, alpha, n_prompt, dec_seq):
    h = jnp.dot(a_ref[...], w_ref[...], preferred_element_type=F32)
    y = _layer_norm(alpha * x_ref[...] + h, g_ref[...], b_ref[...])
    keep = _keep_rows(pl.program_id(0), o_ref.shape[0], n_prompt, dec_seq)
    o_ref[...] = jnp.where(keep, y, 0.0)


def out_proj_ln(a, w_bf16, x, g, b, alpha, n_prompt, dec_seq):
    n, k = a.shape
    d = w_bf16.shape[1]
    tm = _tile(n, ROW_TILE_TARGET, SUBLANES)
    return pl.pallas_call(
        functools.partial(_out_proj_ln_kernel, alpha=alpha, n_prompt=n_prompt, dec_seq=dec_seq),
        grid=(n // tm,),
        in_specs=[pl.BlockSpec((tm, k), lambda i: (i, 0)),
                  pl.BlockSpec((k, d), lambda i: (0, 0)),
                  pl.BlockSpec((tm, d), lambda i: (i, 0)),
                  pl.BlockSpec((1, d), lambda i: (0, 0)),
                  pl.BlockSpec((1, d), lambda i: (0, 0))],
        out_specs=pl.BlockSpec((tm, d), lambda i: (i, 0)),
        out_shape=jax.ShapeDtypeStruct((n, d), F32),
        compiler_params=_params("parallel"),
        name="out_proj_ln",
    )(a, w_bf16, x, g.reshape(1, d), b.reshape(1, d))


def _softplus(z):
    sign = jnp.uint32(0x80000000)
    neg_abs = lax.bitcast_convert_type(lax.bitcast_convert_type(z, jnp.uint32) | sign, F32)
    return jnp.maximum(z, 0.0) + jnp.log(1.0 + jnp.exp(neg_abs))


def _suffix_sum(sp, tri):
    hi = sp.astype(BF16)
    lo = (sp - hi.astype(F32)).astype(BF16)
    return jnp.dot(jnp.concatenate([hi, lo], axis=1), tri, preferred_element_type=F32)


def _tri_incl(n):
    r = lax.broadcasted_iota(jnp.int32, (2 * n, n), 0)
    c = lax.broadcasted_iota(jnp.int32, (2 * n, n), 1)
    return jnp.where(jnp.where(r < n, r, r - n) >= c, 1.0, 0.0).astype(BF16)


def _sb_tile(q2, kbs, vbs, bias_col, carry, tri, masks):
    nt = (((1,), (1,)), ((), ()))
    zs, incls = [], []
    for kb, mask in zip(kbs, masks):
        z = lax.dot_general(q2, kb, nt, preferred_element_type=F32) + bias_col
        sp = _softplus(z)
        if mask is not None:
            sp = jnp.where(mask, sp, 0.0)
        zs.append(z)
        incls.append(_suffix_sum(sp, tri))
    pv = None
    for u in reversed(range(len(kbs))):
        a = jnp.exp(zs[u] - incls[u] - carry)
        if masks[u] is not None:
            a = jnp.where(masks[u], a, 0.0)
        part = jnp.dot(a.astype(BF16), vbs[u], preferred_element_type=F32)
        pv = part if pv is None else pv + part
        carry = carry + incls[u][:, 0:1]
    return pv, carry


def _sb_prompt_kernel(bias_ref, q_ref, k_ref, v_ref, o_in_ref, o_ref, *, head_dim, tq, tk):
    del o_in_ref
    pair = pl.program_id(1)
    n_sub = tk // SB_BLOCK
    first = lax.broadcasted_iota(jnp.int32, (tq, LANES), 1) < head_dim
    rows2 = lax.broadcasted_iota(jnp.int32, (2 * tq, 1), 0)
    bias_col = jnp.where(rows2 < tq, bias_ref[2 * pair], bias_ref[2 * pair + 1])
    tri = _tri_incl(SB_BLOCK)
    r2 = lax.broadcasted_iota(jnp.int32, (2 * tq, SB_BLOCK), 0)
    c_minus_r = lax.broadcasted_iota(jnp.int32, (2 * tq, SB_BLOCK), 1) - jnp.where(r2 < tq, r2, r2 - tq)

    def load(ref, k0, n_blocks=n_sub):
        return [ref[pl.ds(pl.multiple_of(k0 + u * SB_BLOCK, SB_BLOCK), SB_BLOCK), :].astype(BF16)
                for u in range(n_blocks)]

    def q_block(i, _):
        q0 = pl.multiple_of(i * tq, tq)
        q = q_ref[pl.ds(q0, tq), :]
        q2 = jnp.concatenate([jnp.where(first, q, 0.0), jnp.where(first, 0.0, q)], axis=0).astype(BF16)
        n_full = q0 // tk
        k0 = pl.multiple_of(n_full * tk, tk)

        def partial_tile(n_blocks):
            def run():
                masks = [c_minus_r < (q0 - k0 - u * SB_BLOCK) for u in range(n_blocks)]
                return _sb_tile(q2, load(k_ref, k0, n_blocks), load(v_ref, k0, n_blocks), bias_col,
                                jnp.zeros((2 * tq, 1), F32), tri, masks)
            return run

        if tq < tk:
            acc, carry = lax.cond(q0 == k0, partial_tile(tq // SB_BLOCK), partial_tile(n_sub))
        else:
            acc, carry = partial_tile(n_sub)()

        def k_tile(jj, state):
            acc, carry = state
            k0 = pl.multiple_of((n_full - 1 - jj) * tk, tk)
            pv, carry = _sb_tile(q2, load(k_ref, k0), load(v_ref, k0), bias_col, carry, tri,
                                 [None] * n_sub)
            return acc + pv, carry

        acc, _ = lax.fori_loop(0, n_full, k_tile, (acc, carry))
        o_ref[pl.ds(q0, tq), :] = jnp.where(first, acc[:tq], acc[tq:]).astype(o_ref.dtype)
        return 0

    lax.fori_loop(0, q_ref.shape[0] // tq, q_block, 0)


def sb_prompt_attention(qkv, bias, o_init, n_seq, t_len, head_dim):
    _, n, d = qkv.shape
    assert 2 * head_dim == LANES
    tk = min(SB_KEY_TILE, t_len)
    tq = min(SB_QUERY_TILE, tk)
    assert t_len % tk == 0 and tk % tq == 0 and tk % SB_BLOCK == 0
    part = lambda g: pl.BlockSpec((None, t_len, LANES), lambda b, p: (g, b, p))
    return pl.pallas_call(
        functools.partial(_sb_prompt_kernel, head_dim=head_dim, tq=tq, tk=tk),
        grid=(n_seq, d // LANES),
        in_specs=[pl.BlockSpec(memory_space=pltpu.SMEM), part(0), part(1), part(2),
                  pl.BlockSpec(memory_space=pl.ANY)],
        out_specs=pl.BlockSpec((t_len, LANES), lambda b, p: (b, p)),
        out_shape=jax.ShapeDtypeStruct((n, d), BF16),
        input_output_aliases={4: 0},
        compiler_params=_params("parallel", "parallel"),
        name="sb_prompt_attention",
    )(bias, qkv, qkv, qkv, o_init)


def _sb_decode_kernel(pt_ref, bias_ref, q_ref, kn_ref, vn_ref, *rest, pages_per_step, dec_seq):
    del pt_ref
    g = pages_per_step
    k_refs, v_refs = rest[:g], rest[g:2 * g]
    o_ref, acc_sc, carry_sc = rest[2 * g:]
    step = pl.program_id(1)
    n_heads, t_pad, hd = q_ref.shape
    page = k_refs[0].shape[-1]
    n_q = n_heads * t_pad
    tri = _tri_incl(page)
    q3 = q_ref[...].astype(BF16)
    bias_col = bias_ref[...]

    def page_update(kts, vts, masks):
        zs, incls = [], []
        for kt, mask in zip(kts, masks):
            z = jnp.einsum('htd,hds->hts', q3, kt, preferred_element_type=F32).reshape(n_q, page)
            z = z + bias_col
            sp = _softplus(z)
            if mask is not None:
                sp = jnp.where(mask, sp, 0.0)
            zs.append(z)
            incls.append(_suffix_sum(sp, tri))
        carry = carry_sc[...]
        acc = acc_sc[...]
        for u in reversed(range(len(kts))):
            a = jnp.exp(zs[u] - incls[u] - carry)
            if masks[u] is not None:
                a = jnp.where(masks[u], a, 0.0)
            a3 = a.reshape(n_heads, t_pad, page).astype(BF16)
            acc = acc + jnp.einsum('hts,hds->htd', a3, vts[u], preferred_element_type=F32)
            carry = carry + incls[u][:, 0:1]
        carry_sc[...] = carry
        acc_sc[...] = acc

    @pl.when(step == 0)
    def _():
        acc_sc[...] = jnp.zeros_like(acc_sc)
        carry_sc[...] = jnp.zeros_like(carry_sc)
        t_query = lax.broadcasted_iota(jnp.int32, (n_q, page), 0) & (t_pad - 1)
        s_key = lax.broadcasted_iota(jnp.int32, (n_q, page), 1)
        page_update([kn_ref[...].astype(BF16)], [vn_ref[...].astype(BF16)],
                    [(s_key < t_query) & (s_key < dec_seq)])

    page_update([kr[...].astype(BF16) for kr in reversed(k_refs)],
                [vr[...].astype(BF16) for vr in reversed(v_refs)], [None] * g)

    @pl.when(step == pl.num_programs(1) - 1)
    def _():
        o_ref[...] = acc_sc[...]


def sb_decode_attention(qkv_s, bias, cache_kt, cache_vt, page_table, layer, dec_seq):
    _, n_seq, t_new, n_heads, hd = qkv_s.shape
    page = cache_kt.shape[-1]
    n_pages = page_table.shape[1]
    g = _tile(n_pages, DECODE_PAGES_PER_STEP, 1)
    t_pad = DECODE_QUERY_ROWS
    assert t_new <= t_pad <= page and t_pad & (t_pad - 1) == 0
    q3 = jnp.pad(jnp.swapaxes(qkv_s[0], 1, 2), ((0, 0), (0, 0), (0, t_pad - t_new), (0, 0)))
    new_t = lambda a: jnp.pad(jnp.transpose(a, (0, 2, 3, 1)), ((0, 0), (0, 0), (0, 0), (0, page - t_new)))
    head = lambda last: pl.BlockSpec((None, n_heads, last[0], last[1]), lambda b, s, pt: (b, 0, 0, 0))

    def page_spec(j):
        return pl.BlockSpec(
            (None, None, n_heads, hd, page),
            lambda b, s, pt: (layer, pt[b * n_pages + (n_pages - 1 - (s * g + j))], 0, 0, 0))

    kv_specs = [page_spec(j) for j in range(g)]
    n_q = n_heads * t_pad
    out = pl.pallas_call(
        functools.partial(_sb_decode_kernel, pages_per_step=g, dec_seq=dec_seq),
        grid_spec=pltpu.PrefetchScalarGridSpec(
            num_scalar_prefetch=1,
            grid=(n_seq, n_pages // g),
            in_specs=[pl.BlockSpec((n_q, 1), lambda b, s, pt: (0, 0)), head((t_pad, hd)),
                      head((hd, page)), head((hd, page))] + kv_specs + kv_specs,
            out_specs=head((t_pad, hd)),
            scratch_shapes=[pltpu.VMEM((n_heads, t_pad, hd), F32), pltpu.VMEM((n_q, 1), F32)]),
        out_shape=jax.ShapeDtypeStruct((n_seq, n_heads, t_pad, hd), F32),
        compiler_params=_params("parallel", "arbitrary"),
        name="sb_decode_attention",
    )(page_table.reshape(-1), jnp.repeat(bias, t_pad).reshape(n_q, 1), q3, new_t(qkv_s[1]),
      new_t(qkv_s[2]), *([cache_kt] * g), *([cache_vt] * g))
    return jnp.swapaxes(out[:, :, :t_new], 1, 2).reshape(n_seq, t_new, n_heads * hd)


def _rope(a, cos, sin):
    half = a.shape[-1] // 2
    a1, a2 = a[:, :half], a[:, half:]
    return jnp.concatenate([a1 * cos - a2 * sin, a1 * sin + a2 * cos], axis=-1)


def _retention_chunk(p_ref, cos, sin, s_sc, store, *, n_heads, dk, dv, valid):
    c_len = p_ref.shape[1]
    qk_w = n_heads * dk
    idx = lax.broadcasted_iota(jnp.int32, (c_len, 1), 0).astype(F32)
    rel = (lax.broadcasted_iota(jnp.int32, (c_len, c_len), 0)
           - lax.broadcasted_iota(jnp.int32, (c_len, c_len), 1)).astype(F32)
    live = lax.broadcasted_iota(jnp.int32, (c_len, 1), 0) < valid
    nt = (((1,), (1,)), ((), ()))
    tn = (((0,), (0,)), ((), ()))
    for h in range(n_heads):
        log_g = math.log1p(-2.0 ** (-5.0 - h))
        q = _rope(p_ref[0, :, h * dk:(h + 1) * dk], cos, sin)
        k = _rope(p_ref[0, :, qk_w + h * dk:qk_w + (h + 1) * dk], cos, sin) * (dk ** -0.5)
        k = jnp.where(live, k, 0.0)
        v = p_ref[1, :, h * dv:(h + 1) * dv].astype(BF16)
        gate = p_ref[2, :, h * dv:(h + 1) * dv]
        intra = jnp.where(rel >= 0.0, jnp.exp(log_g * jnp.maximum(rel, 0.0)), 0.0)
        q_dec = jnp.exp(log_g * (idx + 1.0))
        k_dec = jnp.exp(log_g * (valid - 1.0 - idx))
        qb = q.astype(BF16)
        sc = lax.dot_general(qb, k.astype(BF16), nt, preferred_element_type=F32) * intra
        s_old = s_sc[h]
        o = (jnp.dot(sc.astype(BF16), v, preferred_element_type=F32)
             + jnp.dot(qb, s_old.astype(BF16), preferred_element_type=F32) * q_dec)
        s_sc[h] = (math.exp(log_g * valid) * s_old
                   + lax.dot_general((k * k_dec).astype(BF16), v, tn, preferred_element_type=F32))
        mu = jnp.mean(o, axis=-1, keepdims=True)
        oc = o - mu
        var = jnp.mean(oc * oc, axis=-1, keepdims=True)
        on = oc * lax.rsqrt(var + LN_EPS)
        store(h, gate * (1.0 / (1.0 + jnp.exp(-gate))) * on)


def _ret_prompt_kernel(p_ref, cos_ref, sin_ref, y_in_ref, y_ref, s_out_ref, s_sc, *, n_heads, dk, dv):
    del y_in_ref
    c = pl.program_id(1)

    @pl.when(c == 0)
    def _():
        s_sc[...] = jnp.zeros_like(s_sc)

    def store(h, val):
        y_ref[:, h * dv:(h + 1) * dv] = val.astype(y_ref.dtype)

    _retention_chunk(p_ref, cos_ref[...], sin_ref[...], s_sc, store,
                     n_heads=n_heads, dk=dk, dv=dv, valid=p_ref.shape[1])

    @pl.when(c == pl.num_programs(1) - 1)
    def _():
        s_out_ref[...] = s_sc[...]


def retention_prompt(p, cos, sin, y_init, n_seq, t_len, n_heads, dk, dv):
    _, n, v_w = p.shape
    assert v_w == n_heads * dv == 2 * n_heads * dk
    chunk = _tile(t_len, RET_CHUNK, SUBLANES)
    n_chunks = t_len // chunk
    tab = pl.BlockSpec((chunk, dk // 2), lambda b, c: (c, 0))
    return pl.pallas_call(
        functools.partial(_ret_prompt_kernel, n_heads=n_heads, dk=dk, dv=dv),
        grid=(n_seq, n_chunks),
        in_specs=[pl.BlockSpec((3, chunk, v_w), lambda b, c: (0, b * n_chunks + c, 0)),
                  tab, tab, pl.BlockSpec(memory_space=pl.ANY)],
        out_specs=[pl.BlockSpec((chunk, v_w), lambda b, c: (b * n_chunks + c, 0)),
                   pl.BlockSpec((None, n_heads, dk, dv), lambda b, c: (b, 0, 0, 0))],
        out_shape=[jax.ShapeDtypeStruct((n, v_w), BF16),
                   jax.ShapeDtypeStruct((n_seq, n_heads, dk, dv), F32)],
        scratch_shapes=[pltpu.VMEM((n_heads, dk, dv), F32)],
        input_output_aliases={3: 0},
        compiler_params=_params("parallel", "arbitrary"),
        name="retention_prompt",
    )(p, cos, sin, y_init)


def _ret_decode_kernel(p_ref, cos_ref, sin_ref, s_in_ref, y_in_ref, y_ref, s_out_ref, p_sc, s_sc,
                       *, n_heads, dk, dv, dec_seq):
    del y_in_ref
    p_sc[...] = jnp.zeros_like(p_sc)
    p_sc[:, 0:SAMPLE_PAD, :] = p_ref[...]
    s_sc[...] = s_in_ref[...]

    def store(h, val):
        y_ref[:, h * dv:(h + 1) * dv] = val[0:SAMPLE_PAD].astype(y_ref.dtype)

    _retention_chunk(p_sc, cos_ref[...], sin_ref[...], s_sc, store,
                     n_heads=n_heads, dk=dk, dv=dv, valid=dec_seq)
    s_out_ref[...] = s_sc[...]


def retention_decode(p, cos, sin, state, layer, y_init, n_prompt, n_heads, dk, dv, dec_seq):
    _, n, v_w = p.shape
    n_seq = state.shape[1]
    base = n_prompt // SAMPLE_PAD
    tab = pl.BlockSpec((RET_DECODE_ROWS, dk // 2), lambda b: (0, 0))
    return pl.pallas_call(
        functools.partial(_ret_decode_kernel, n_heads=n_heads, dk=dk, dv=dv, dec_seq=dec_seq),
        grid=(n_seq,),
        in_specs=[pl.BlockSpec((3, SAMPLE_PAD, v_w), lambda b: (0, base + b, 0)), tab, tab,
                  pl.BlockSpec((None, None, n_heads, dk, dv), lambda b: (layer, b, 0, 0, 0)),
                  pl.BlockSpec(memory_space=pl.ANY)],
        out_specs=[pl.BlockSpec((SAMPLE_PAD, v_w), lambda b: (base + b, 0)),
                   pl.BlockSpec((None, n_heads, dk, dv), lambda b: (b, 0, 0, 0))],
        out_shape=[jax.ShapeDtypeStruct((n, v_w), BF16),
                   jax.ShapeDtypeStruct((n_seq, n_heads, dk, dv), F32)],
        scratch_shapes=[pltpu.VMEM((3, RET_DECODE_ROWS, v_w), F32),
                        pltpu.VMEM((n_heads, dk, dv), F32)],
        input_output_aliases={4: 0},
        compiler_params=_params("parallel"),
        name="retention_decode",
    )(p, cos, sin, state, y_init)


def _split_bf16(a):
    hi = a.astype(BF16)
    return hi, (a - hi.astype(F32)).astype(BF16)


def _router_kernel(x_ref, wt_ref, b_ref, idx_ref, gate_ref, rank_ref, cnt_ref, base_sc):
    n_exp = wt_ref.shape[0]
    tr = x_ref.shape[0]

    @pl.when(pl.program_id(0) == 0)
    def _():
        base_sc[...] = jnp.zeros_like(base_sc)

    nt = (((1,), (1,)), ((), ()))
    xh, xl = _split_bf16(x_ref[...])
    wh, wl = _split_bf16(wt_ref[...])
    logits = (lax.dot_general(wh, xh, nt, preferred_element_type=F32)
              + lax.dot_general(wh, xl, nt, preferred_element_type=F32)
              + lax.dot_general(wl, xh, nt, preferred_element_type=F32)) + b_ref[...]
    eidx = lax.broadcasted_iota(jnp.int32, (n_exp, tr), 0)
    r = lax.broadcasted_iota(jnp.int32, (tr, tr), 0)
    c = lax.broadcasted_iota(jnp.int32, (tr, tr), 1)
    before = jnp.where(r < c, 1.0, 0.0).astype(BF16)
    work = logits
    vals, sels = [], []
    seen = base_sc[...]
    for kk in range(TOP_K):
        m = jnp.max(work, axis=0, keepdims=True)
        sel = jnp.min(jnp.where(work == m, eidx, n_exp), axis=0, keepdims=True)
        hit = eidx == sel
        work = jnp.where(hit, -jnp.inf, work)
        onehot = jnp.where(hit, 1.0, 0.0)
        prefix = jnp.dot(onehot.astype(BF16), before, preferred_element_type=F32)
        rank = jnp.sum(onehot * (seen + prefix), axis=0, keepdims=True)
        rank_ref[kk:kk + 1, :] = rank.astype(jnp.int32)
        seen = seen + jnp.sum(onehot, axis=1, keepdims=True)
        vals.append(m)
        sels.append(sel)
    base_sc[...] = seen
    cnt_ref[...] = jnp.broadcast_to(seen, cnt_ref.shape)
    es = [jnp.exp(v - vals[0]) for v in vals]
    tot = es[0] + es[1] + es[2] + es[3]
    for kk in range(TOP_K):
        idx_ref[kk:kk + 1, :] = sels[kk]
        gate_ref[kk:kk + 1, :] = es[kk] / tot


def moe_router(x, w_router, b_router):
    n, d = x.shape
    n_exp = w_router.shape[1]
    tr = _tile(n, ROUTER_TILE_TARGET, LANES)
    tok = pl.BlockSpec((TOP_K, tr), lambda i: (0, i))
    idx, gates, rank, cnt = pl.pallas_call(
        _router_kernel,
        grid=(n // tr,),
        in_specs=[pl.BlockSpec((tr, d), lambda i: (i, 0)),
                  pl.BlockSpec((n_exp, d), lambda i: (0, 0)),
                  pl.BlockSpec((n_exp, 1), lambda i: (0, 0))],
        out_specs=[tok, tok, tok, pl.BlockSpec((n_exp, LANES), lambda i: (0, 0))],
        out_shape=[jax.ShapeDtypeStruct((TOP_K, n), jnp.int32),
                   jax.ShapeDtypeStruct((TOP_K, n), F32),
                   jax.ShapeDtypeStruct((TOP_K, n), jnp.int32),
                   jax.ShapeDtypeStruct((n_exp, LANES), F32)],
        scratch_shapes=[pltpu.VMEM((n_exp, 1), F32)],
        compiler_params=_params("arbitrary"),
        name="moe_router",
    )(x, w_router.T, b_router.reshape(n_exp, 1))
    return idx, gates, rank, cnt[:, 0].astype(jnp.int32)


def _deinterleave_matrix():
    r = lax.broadcasted_iota(jnp.int32, (MXU_DIM, MXU_DIM), 0)
    c = lax.broadcasted_iota(jnp.int32, (MXU_DIM, MXU_DIM), 1)
    src = jnp.where(c < MXU_DIM // 2, 2 * c, 2 * (c - MXU_DIM // 2) + 1)
    return jnp.where(r == src, 1.0, 0.0).astype(BF16)


def _expert_kernel(be_ref, nv_ref, x_ref, wu_ref, bu_ref, wd_ref, bd_ref, o_ref, wu_sc, wd_sc, glu_sc):
    i = pl.program_id(0)
    valid = i < nv_ref[0]
    fresh = (i == 0) | (be_ref[i] != be_ref[jnp.maximum(i - 1, 0)])
    n_chunks = wu_sc.shape[1] // MXU_DIM
    half = MXU_DIM // 2

    @pl.when(valid & fresh)
    def _():
        perm = _deinterleave_matrix()
        for cc in range(n_chunks):
            cols = slice(cc * MXU_DIM, (cc + 1) * MXU_DIM)
            wu_sc[:, cols] = jnp.dot(wu_ref[:, cols].astype(BF16), perm,
                                     preferred_element_type=F32).astype(BF16)
        wd_sc[...] = wd_ref[...].astype(BF16)

    @pl.when(valid)
    def _():
        xb = x_ref[...].astype(BF16)
        for cc in range(n_chunks):
            cols = slice(cc * MXU_DIM, (cc + 1) * MXU_DIM)
            h = jnp.dot(xb, wu_sc[:, cols], preferred_element_type=F32) + bu_ref[:, cols]
            gate = jnp.minimum(h[:, :half], SWIGLU_LIMIT)
            up = jnp.clip(h[:, half:], -SWIGLU_LIMIT, SWIGLU_LIMIT)
            glu = gate * (1.0 / (1.0 + jnp.exp(-SWIGLU_ALPHA * gate)))
            glu_sc[:, cc * half:(cc + 1) * half] = ((up + 1.0) * glu).astype(BF16)
        y = jnp.dot(glu_sc[...], wd_sc[...], preferred_element_type=F32) + bd_ref[...]
        o_ref[...] = y.reshape(o_ref.shape)

    @pl.when(jnp.logical_not(valid))
    def _():
        o_ref[...] = jnp.zeros_like(o_ref)


def moe_experts(xs, block_e, n_valid, w_up, b_up_perm, w_down, b_down, layer):
    n_rows, d = xs.shape
    bm = MOE_BLOCK_ROWS
    n_exp, _, two_f = w_up.shape[1:]
    f = two_f // 2
    assert two_f % MXU_DIM == 0
    return pl.pallas_call(
        _expert_kernel,
        grid_spec=pltpu.PrefetchScalarGridSpec(
            num_scalar_prefetch=2,
            grid=(n_rows // bm,),
            in_specs=[pl.BlockSpec((bm, d), lambda i, be, nv: (i, 0)),
                      pl.BlockSpec((None, None, d, two_f), lambda i, be, nv: (layer, be[i], 0, 0)),
                      pl.BlockSpec((None, None, 1, two_f), lambda i, be, nv: (layer, be[i], 0, 0)),
                      pl.BlockSpec((None, None, f, d), lambda i, be, nv: (layer, be[i], 0, 0)),
                      pl.BlockSpec((None, None, 1, d), lambda i, be, nv: (layer, be[i], 0, 0))],
            out_specs=pl.BlockSpec((bm, d // LANES, LANES), lambda i, be, nv: (i, 0, 0)),
            scratch_shapes=[pltpu.VMEM((d, two_f), BF16), pltpu.VMEM((f, d), BF16),
                            pltpu.VMEM((bm, f), BF16)]),
        out_shape=jax.ShapeDtypeStruct((n_rows, d // LANES, LANES), F32),
        compiler_params=_params("arbitrary"),
        name="moe_experts",
    )(block_e, n_valid, xs, w_up, b_up_perm, w_down, b_down)


def _deinterleave_bias(b_up):
    l, e, two_f = b_up.shape
    b = b_up.reshape(l, e, two_f // MXU_DIM, MXU_DIM // 2, 2)
    return jnp.swapaxes(b, -1, -2).reshape(l, e, 1, two_f)


def sc_gather_rows(table, idx):
    n_idx = idx.shape[0]
    row = table.shape[1:]
    n_workers = SC_CORES * SC_SUBCORES
    assert n_idx % (n_workers * SUBLANES) == 0
    per_worker = n_idx // n_workers
    chunk = _tile(per_worker, SC_GATHER_ROWS, SUBLANES)
    mesh = plsc.VectorSubcoreMesh(core_axis_name="c", subcore_axis_name="s",
                                  num_cores=SC_CORES, num_subcores=SC_SUBCORES)

    n_chunks = per_worker // chunk

    @functools.partial(
        pl.kernel, mesh=mesh,
        out_type=jax.ShapeDtypeStruct((n_idx,) + row, table.dtype),
        scratch_types=[pltpu.VMEM((chunk,), jnp.int32), pltpu.VMEM((chunk,), jnp.int32),
                       pltpu.VMEM((chunk,) + row, table.dtype), pltpu.VMEM((chunk,) + row, table.dtype),
                       pltpu.SemaphoreType.DMA, pltpu.SemaphoreType.DMA])
    def gather(table_hbm, idx_hbm, out_hbm, idx0, idx1, rows0, rows1, sem0, sem1):
        worker = lax.axis_index("s") * SC_CORES + lax.axis_index("c")
        base = worker * per_worker
        bufs = ((idx0, rows0, sem0), (idx1, rows1, sem1))

        def start(c, buf):
            idx_v, rows_v, sem = buf
            off = pl.multiple_of(base + c * chunk, SUBLANES)
            pltpu.sync_copy(idx_hbm.at[pl.ds(off, chunk)], idx_v)
            pltpu.async_copy(table_hbm.at[idx_v], rows_v, sem)

        def finish(c, buf):
            idx_v, rows_v, sem = buf
            pltpu.make_async_copy(table_hbm.at[idx_v], rows_v, sem).wait()
            off = pl.multiple_of(base + c * chunk, SUBLANES)
            pltpu.sync_copy(rows_v, out_hbm.at[pl.ds(off, chunk)])

        start(0, bufs[0])

        @pl.loop(0, n_chunks, step=2)
        def _(c):
            for b in range(2):
                @pl.when(c + b + 1 < n_chunks)
                def _():
                    start(c + b + 1, bufs[1 - b])

                @pl.when(c + b < n_chunks)
                def _():
                    finish(c + b, bufs[b])

    return gather(table, idx)


def _combine_ln_kernel(x_ref, y_ref, gate_ref, g_ref, b_ref, o_ref, *, alpha, n_prompt, dec_seq):
    gates = gate_ref[...]
    rows = x_ref.shape
    f = gates[:, 0:1] * y_ref[0].reshape(rows)
    for kk in range(1, TOP_K):
        f = f + gates[:, kk:kk + 1] * y_ref[kk].reshape(rows)
    y = _layer_norm(alpha * x_ref[...] + f, g_ref[...], b_ref[...])
    keep = _keep_rows(pl.program_id(0), o_ref.shape[0], n_prompt, dec_seq)
    o_ref[...] = jnp.where(keep, y, 0.0)


def combine_ln(x, y4, gates_t, g, b, alpha, n_prompt, dec_seq):
    n, d = x.shape
    tm = _tile(n, ROW_TILE_TARGET // 2, SUBLANES)
    row = pl.BlockSpec((tm, d), lambda i: (i, 0))
    vec = pl.BlockSpec((1, d), lambda i: (0, 0))
    return pl.pallas_call(
        functools.partial(_combine_ln_kernel, alpha=alpha, n_prompt=n_prompt, dec_seq=dec_seq),
        grid=(n // tm,),
        in_specs=[row, pl.BlockSpec((TOP_K, tm, d // LANES, LANES), lambda i: (0, i, 0, 0)),
                  pl.BlockSpec((tm, TOP_K), lambda i: (i, 0)), vec, vec],
        out_specs=row,
        out_shape=jax.ShapeDtypeStruct((n, d), F32),
        compiler_params=_params("parallel"),
        name="combine_ln",
    )(x, y4, gates_t, g.reshape(1, d), b.reshape(1, d))


def moe_ffn_ln(x, layer, w_router, b_router, w_up, b_up_perm, w_down, b_down, g, b,
               alpha, n_prompt, dec_seq):
    n, d = x.shape
    n_exp = w_router.shape[1]
    bm = MOE_BLOCK_ROWS
    idx, gates, rank, counts = moe_router(x, w_router, b_router)
    n_blocks = -(-(n * TOP_K + n_exp * (bm - 1)) // bm)
    padded = (counts + bm - 1) // bm * bm
    pad_end = jnp.cumsum(padded)
    pad_start = pad_end - padded
    n_valid = pad_end[-1] // bm
    blk_start = jnp.minimum(jnp.arange(n_blocks, dtype=jnp.int32), n_valid - 1) * bm
    block_e = jnp.minimum(jnp.sum(pad_end[None, :] <= blk_start[:, None], axis=1),
                          n_exp - 1).astype(jnp.int32)
    experts = jnp.arange(n_exp, dtype=jnp.int32)[:, None, None]
    dest = rank + jnp.sum(jnp.where(idx[None] == experts, pad_start[:, None, None], 0), axis=0)
    tok = jnp.broadcast_to(jnp.arange(n, dtype=jnp.int32)[None, :], (TOP_K, n))
    rows = jnp.zeros((n_blocks * bm,), jnp.int32).at[dest.reshape(-1)].set(tok.reshape(-1))
    xs = sc_gather_rows(x, rows)
    ys = moe_experts(xs, block_e, n_valid.reshape(1).astype(jnp.int32), w_up, b_up_perm,
                     w_down, b_down.reshape(b_down.shape[0], n_exp, 1, d), layer)
    y4 = sc_gather_rows(ys, dest.reshape(-1)).reshape(TOP_K, n, d // LANES, LANES)
    return combine_ln(x, y4, gates.T, g, b, alpha, n_prompt, dec_seq)


def kernel(x_prompt, x_sample, cache_k, cache_v, state_ret, page_table, w_sb_qkv, w_sb_out, sb_bias,
           w_ret_in, w_ret_out, ln_mix_g, ln_mix_b, ln_ffn_g, ln_ffn_b, w_router, b_router,
           w_exp_up, b_exp_up, w_exp_down, b_exp_down):
    n_seq, t_len, d = x_prompt.shape
    dec_batch, dec_seq, _ = x_sample.shape
    depth = ln_mix_g.shape[0]
    page, sb_heads, sb_hd = cache_k.shape[2:]
    ret_heads, dk, dv = state_ret.shape[2:]
    past_len = page_table.shape[1] * page
    alpha = (2 * depth) ** 0.25
    n_prompt = n_seq * t_len
    n_sample = dec_batch * SAMPLE_PAD
    assert dec_seq <= SAMPLE_PAD
    sb_scale = sb_hd ** -0.5
    assert math.log2(sb_scale) == round(math.log2(sb_scale))

    xs_pad = jnp.pad(x_sample, ((0, 0), (0, SAMPLE_PAD - dec_seq), (0, 0)))
    x = jnp.concatenate([x_prompt.reshape(n_prompt, d), xs_pad.reshape(n_sample, d)], axis=0)
    n = n_prompt + n_sample

    half = dk // 2
    inv = ROPE_BASE ** (-jnp.arange(half, dtype=F32) / half)
    ang_p = jnp.arange(t_len, dtype=F32)[:, None] * inv[None, :]
    ang_s = (past_len + jnp.arange(RET_DECODE_ROWS, dtype=F32))[:, None] * inv[None, :]
    b_up_perm = _deinterleave_bias(b_exp_up)
    cache_kt = jnp.transpose(cache_k, (0, 1, 3, 4, 2))
    cache_vt = jnp.transpose(cache_v, (0, 1, 3, 4, 2))

    new_k, new_v, new_s = [], [], []
    for i in range(depth):
        j = i // 2
        if i % 2 == 0:
            qkv = stacked_proj(x, w_sb_qkv[j].astype(BF16), 3, sb_scale)
            qkv_s = qkv[:, n_prompt:].reshape(3, dec_batch, SAMPLE_PAD, sb_heads, sb_hd)
            o_s = sb_decode_attention(qkv_s, sb_bias[j], cache_kt, cache_vt, page_table, j, dec_seq)
            o = jnp.zeros((n, d), BF16).at[n_prompt:].set(o_s.reshape(n_sample, d).astype(BF16))
            o = sb_prompt_attention(qkv, sb_bias[j], o, n_seq, t_len, sb_hd)
            w_out = w_sb_out[j]
            new_k.append(qkv[1])
            new_v.append(qkv[2])
        else:
            p = stacked_proj(x, w_ret_in[j].astype(BF16), 3)
            o = jnp.zeros((n, ret_heads * dv), BF16)
            o, s_s = retention_decode(p, jnp.cos(ang_s), jnp.sin(ang_s), state_ret, j, o,
                                      n_prompt, ret_heads, dk, dv, dec_seq)
            o, s_p = retention_prompt(p, jnp.cos(ang_p), jnp.sin(ang_p), o, n_seq, t_len,
                                      ret_heads, dk, dv)
            w_out = w_ret_out[j]
            new_s.append((s_p, s_s))
        x = out_proj_ln(o, w_out.astype(BF16), x, ln_mix_g[i], ln_mix_b[i], alpha, n_prompt, dec_seq)
        x = moe_ffn_ln(x, i, w_router[i], b_router[i], w_exp_up, b_up_perm, w_exp_down, b_exp_down,
                       ln_ffn_g[i], ln_ffn_b[i], alpha, n_prompt, dec_seq)

    def prompt_rows(a):
        return a[:n_prompt].reshape(n_seq, t_len, sb_heads, sb_hd)

    def sample_rows(a):
        return a[n_prompt:].reshape(dec_batch, SAMPLE_PAD, -1)[:, :dec_seq]

    y_prompt = x[:n_prompt].reshape(n_seq, t_len, d)
    y_sample = sample_rows(x)
    k_prompt = jnp.stack([prompt_rows(a) for a in new_k])
    v_prompt = jnp.stack([prompt_rows(a) for a in new_v])
    k_sample = jnp.stack([sample_rows(a).reshape(dec_batch, dec_seq, sb_heads, sb_hd) for a in new_k])
    v_sample = jnp.stack([sample_rows(a).reshape(dec_batch, dec_seq, sb_heads, sb_hd) for a in new_v])
    state_prompt = jnp.stack([s[0] for s in new_s])
    state_sample = jnp.stack([s[1] for s in new_s])
    return (y_prompt, y_sample, k_prompt, v_prompt, state_prompt, k_sample, v_sample, state_sample)
```

```python
import functools
import math

import jax
import jax.numpy as jnp
from jax import lax
from jax.experimental import pallas as pl
from jax.experimental.pallas import tpu as pltpu
from jax.experimental.pallas import tpu_sc as plsc

F32 = jnp.float32
BF16 = jnp.bfloat16

TOP_K = 4
ROPE_BASE = 10000.0
LN_EPS = 1e-5
SWIGLU_LIMIT = 7.0
SWIGLU_ALPHA = 1.702

LANES = 128
SUBLANES = 8
MXU_DIM = 256
SAMPLE_PAD = SUBLANES
VMEM_LIMIT_BYTES = 56 * 1024 * 1024
ROW_TILE_TARGET = 1024
MOE_BLOCK_ROWS = 256
SB_BLOCK = LANES
SB_QUERY_TILE = 512
SB_KEY_TILE = 512
RET_CHUNK = 256
RET_DECODE_ROWS = 128
ROUTER_TILE_TARGET = 640
DECODE_PAGES_PER_STEP = 8
DECODE_QUERY_ROWS = 16
SC_CORES = 2
SC_SUBCORES = 16
SC_GATHER_ROWS = 16
SC_RING = 4


def _params(*sem):
    return pltpu.CompilerParams(dimension_semantics=sem, vmem_limit_bytes=VMEM_LIMIT_BYTES)


def _tile(n, target, unit):
    best = None
    for t in range(unit, min(n, target) + 1, unit):
        if n % t == 0:
            best = t
    assert best is not None, (n, target, unit)
    return best


def _layer_norm(y, g, b):
    mu = jnp.mean(y, axis=-1, keepdims=True)
    yc = y - mu
    var = jnp.mean(yc * yc, axis=-1, keepdims=True)
    return yc * lax.rsqrt(var + LN_EPS) * g + b


def _keep_rows(tile_idx, tm, n_prompt, dec_seq):
    row = tile_idx * tm + lax.broadcasted_iota(jnp.int32, (tm, 1), 0)
    pad = (row >= n_prompt) & (((row - n_prompt) & (SAMPLE_PAD - 1)) >= dec_seq)
    return jnp.logical_not(pad)


def _stacked_proj_kernel(x_ref, w_ref, o_ref, *, first_scale):
    y = jnp.dot(x_ref[...].astype(BF16), w_ref[...], preferred_element_type=F32)
    if first_scale != 1.0:
        y = y * jnp.where(pl.program_id(1) == 0, first_scale, 1.0)
    o_ref[...] = y


def stacked_proj(x, w_bf16, n_groups, first_scale=1.0):
    n, k = x.shape
    width = w_bf16.shape[1] // n_groups
    tm = _tile(n, ROW_TILE_TARGET, SUBLANES)
    return pl.pallas_call(
        functools.partial(_stacked_proj_kernel, first_scale=first_scale),
        grid=(n // tm, n_groups),
        in_specs=[pl.BlockSpec((tm, k), lambda i, j: (i, 0)),
                  pl.BlockSpec((k, width), lambda i, j: (0, j))],
        out_specs=pl.BlockSpec((None, tm, width), lambda i, j: (j, i, 0)),
        out_shape=jax.ShapeDtypeStruct((n_groups, n, width), F32),
        compiler_params=_params("parallel", "arbitrary"),
        name="stacked_proj",
    )(x, w_bf16)


def _out_proj_ln_kernel(a_ref, w_ref, x_ref, g_ref, b_ref, o_ref, *, alpha, n_prompt, dec_seq):
    h = jnp.dot(a_ref[...], w_ref[...], preferred_element_type=F32)
    y = _layer_norm(alpha * x_ref[...] + h, g_ref[...], b_ref[...])
    keep = _keep_rows(pl.program_id(0), o_ref.shape[0], n_prompt, dec_seq)
    o_ref[...] = jnp.where(keep, y, 0.0)


def out_proj_ln(a, w_bf16, x, g, b, alpha, n_prompt, dec_seq):
    n, k = a.shape
    d = w_bf16.shape[1]
    tm = _tile(n, ROW_TILE_TARGET, SUBLANES)
    return pl.pallas_call(
        functools.partial(_out_proj_ln_kernel, alpha=alpha, n_prompt=n_prompt, dec_seq=dec_seq),
        grid=(n // tm,),
        in_specs=[pl.BlockSpec((tm, k), lambda i: (i, 0)),
                  pl.BlockSpec((k, d), lambda i: (0, 0)),
                  pl.BlockSpec((tm, d), lambda i: (i, 0)),
                  pl.BlockSpec((1, d), lambda i: (0, 0)),
                  pl.BlockSpec((1, d), lambda i: (0, 0))],
        out_specs=pl.BlockSpec((tm, d), lambda i: (i, 0)),
        out_shape=jax.ShapeDtypeStruct((n, d), F32),
        compiler_params=_params("parallel"),
        name="out_proj_ln",
    )(a, w_bf16, x, g.reshape(1, d), b.reshape(1, d))


def _softplus(z):
    sign = jnp.uint32(0x80000000)
    neg_abs = lax.bitcast_convert_type(lax.bitcast_convert_type(z, jnp.uint32) | sign, F32)
    return jnp.maximum(z, 0.0) + jnp.log(1.0 + jnp.exp(neg_abs))


def _suffix_sum(sp, tri):
    hi = sp.astype(BF16)
    lo = (sp - hi.astype(F32)).astype(BF16)
    return jnp.dot(jnp.concatenate([hi, lo], axis=1), tri, preferred_element_type=F32)


def _tri_incl(n):
    r = lax.broadcasted_iota(jnp.int32, (2 * n, n), 0)
    c = lax.broadcasted_iota(jnp.int32, (2 * n, n), 1)
    return jnp.where(jnp.where(r < n, r, r - n) >= c, 1.0, 0.0).astype(BF16)


def _sb_tile(q2, kbs, vbs, bias_col, carry, tri, masks):
    nt = (((1,), (1,)), ((), ()))
    zs, incls = [], []
    for kb, mask in zip(kbs, masks):
        z = lax.dot_general(q2, kb, nt, preferred_element_type=F32) + bias_col
        sp = _softplus(z)
        if mask is not None:
            sp = jnp.where(mask, sp, 0.0)
        zs.append(z)
        incls.append(_suffix_sum(sp, tri))
    pv = None
    for u in reversed(range(len(kbs))):
        a = jnp.exp(zs[u] - incls[u] - carry)
        if masks[u] is not None:
            a = jnp.where(masks[u], a, 0.0)
        part = jnp.dot(a.astype(BF16), vbs[u], preferred_element_type=F32)
        pv = part if pv is None else pv + part
        carry = carry + incls[u][:, 0:1]
    return pv, carry


def _sb_prompt_kernel(bias_ref, q_ref, k_ref, v_ref, o_in_ref, o_ref, *, head_dim, tq, tk):
    del o_in_ref
    pair = pl.program_id(1)
    n_sub = tk // SB_BLOCK
    first = lax.broadcasted_iota(jnp.int32, (tq, LANES), 1) < head_dim
    rows2 = lax.broadcasted_iota(jnp.int32, (2 * tq, 1), 0)
    bias_col = jnp.where(rows2 < tq, bias_ref[2 * pair], bias_ref[2 * pair + 1])
    tri = _tri_incl(SB_BLOCK)
    r2 = lax.broadcasted_iota(jnp.int32, (2 * tq, SB_BLOCK), 0)
    c_minus_r = lax.broadcasted_iota(jnp.int32, (2 * tq, SB_BLOCK), 1) - jnp.where(r2 < tq, r2, r2 - tq)

    def load(ref, k0, n_blocks=n_sub):
        return [ref[pl.ds(pl.multiple_of(k0 + u * SB_BLOCK, SB_BLOCK), SB_BLOCK), :].astype(BF16)
                for u in range(n_blocks)]

    def q_block(i, _):
        q0 = pl.multiple_of(i * tq, tq)
        q = q_ref[pl.ds(q0, tq), :]
        q2 = jnp.concatenate([jnp.where(first, q, 0.0), jnp.where(first, 0.0, q)], axis=0).astype(BF16)
        n_full = q0 // tk
        k0 = pl.multiple_of(n_full * tk, tk)

        def partial_tile(n_blocks):
            def run():
                masks = [c_minus_r < (q0 - k0 - u * SB_BLOCK) for u in range(n_blocks)]
                return _sb_tile(q2, load(k_ref, k0, n_blocks), load(v_ref, k0, n_blocks), bias_col,
                                jnp.zeros((2 * tq, 1), F32), tri, masks)
            return run

        if tq < tk:
            acc, carry = lax.cond(q0 == k0, partial_tile(tq // SB_BLOCK), partial_tile(n_sub))
        else:
            acc, carry = partial_tile(n_sub)()

        def k_tile(jj, state):
            acc, carry = state
            k0 = pl.multiple_of((n_full - 1 - jj) * tk, tk)
            pv, carry = _sb_tile(q2, load(k_ref, k0), load(v_ref, k0), bias_col, carry, tri,
                                 [None] * n_sub)
            return acc + pv, carry

        acc, _ = lax.fori_loop(0, n_full, k_tile, (acc, carry))
        o_ref[pl.ds(q0, tq), :] = jnp.where(first, acc[:tq], acc[tq:]).astype(o_ref.dtype)
        return 0

    lax.fori_loop(0, q_ref.shape[0] // tq, q_block, 0)


def sb_prompt_attention(qkv, bias, o_init, n_seq, t_len, head_dim):
    _, n, d = qkv.shape
    assert 2 * head_dim == LANES
    tk = min(SB_KEY_TILE, t_len)
    tq = min(SB_QUERY_TILE, tk)
    assert t_len % tk == 0 and tk % tq == 0 and tk % SB_BLOCK == 0
    part = lambda g: pl.BlockSpec((None, t_len, LANES), lambda b, p: (g, b, p))
    return pl.pallas_call(
        functools.partial(_sb_prompt_kernel, head_dim=head_dim, tq=tq, tk=tk),
        grid=(n_seq, d // LANES),
        in_specs=[pl.BlockSpec(memory_space=pltpu.SMEM), part(0), part(1), part(2),
                  pl.BlockSpec(memory_space=pl.ANY)],
        out_specs=pl.BlockSpec((t_len, LANES), lambda b, p: (b, p)),
        out_shape=jax.ShapeDtypeStruct((n, d), BF16),
        input_output_aliases={4: 0},
        compiler_params=_params("parallel", "parallel"),
        name="sb_prompt_attention",
    )(bias, qkv, qkv, qkv, o_init)


def _sb_decode_kernel(pt_ref, bias_ref, q_ref, kn_ref, vn_ref, *rest, pages_per_step, dec_seq):
    del pt_ref
    g = pages_per_step
    k_refs, v_refs = rest[:g], rest[g:2 * g]
    o_ref, acc_sc, carry_sc = rest[2 * g:]
    step = pl.program_id(1)
    n_heads, t_pad, hd = q_ref.shape
    page = k_refs[0].shape[-1]
    n_q = n_heads * t_pad
    tri = _tri_incl(page)
    q3 = q_ref[...].astype(BF16)
    bias_col = bias_ref[...]

    def page_update(kts, vts, masks):
        zs, incls = [], []
        for kt, mask in zip(kts, masks):
            z = jnp.einsum('htd,hds->hts', q3, kt, preferred_element_type=F32).reshape(n_q, page)
            z = z + bias_col
            sp = _softplus(z)
            if mask is not None:
                sp = jnp.where(mask, sp, 0.0)
            zs.append(z)
            incls.append(_suffix_sum(sp, tri))
        carry = carry_sc[...]
        acc = acc_sc[...]
        for u in reversed(range(len(kts))):
            a = jnp.exp(zs[u] - incls[u] - carry)
            if masks[u] is not None:
                a = jnp.where(masks[u], a, 0.0)
            a3 = a.reshape(n_heads, t_pad, page).astype(BF16)
            acc = acc + jnp.einsum('hts,hds->htd', a3, vts[u], preferred_element_type=F32)
            carry = carry + incls[u][:, 0:1]
        carry_sc[...] = carry
        acc_sc[...] = acc

    @pl.when(step == 0)
    def _():
        acc_sc[...] = jnp.zeros_like(acc_sc)
        carry_sc[...] = jnp.zeros_like(carry_sc)
        t_query = lax.broadcasted_iota(jnp.int32, (n_q, page), 0) & (t_pad - 1)
        s_key = lax.broadcasted_iota(jnp.int32, (n_q, page), 1)
        page_update([kn_ref[...].astype(BF16)], [vn_ref[...].astype(BF16)],
                    [(s_key < t_query) & (s_key < dec_seq)])

    page_update([kr[...].astype(BF16) for kr in reversed(k_refs)],
                [vr[...].astype(BF16) for vr in reversed(v_refs)], [None] * g)

    @pl.when(step == pl.num_programs(1) - 1)
    def _():
        o_ref[...] = acc_sc[...]


def sb_decode_attention(qkv_s, bias, cache_kt, cache_vt, page_table, layer, dec_seq):
    _, n_seq, t_new, n_heads, hd = qkv_s.shape
    page = cache_kt.shape[-1]
    n_pages = page_table.shape[1]
    g = _tile(n_pages, DECODE_PAGES_PER_STEP, 1)
    t_pad = DECODE_QUERY_ROWS
    assert t_new <= t_pad <= page and t_pad & (t_pad - 1) == 0
    q3 = jnp.pad(jnp.swapaxes(qkv_s[0], 1, 2), ((0, 0), (0, 0), (0, t_pad - t_new), (0, 0)))
    new_t = lambda a: jnp.pad(jnp.transpose(a, (0, 2, 3, 1)), ((0, 0), (0, 0), (0, 0), (0, page - t_new)))
    head = lambda last: pl.BlockSpec((None, n_heads, last[0], last[1]), lambda b, s, pt: (b, 0, 0, 0))

    def page_spec(j):
        return pl.BlockSpec(
            (None, None, n_heads, hd, page),
            lambda b, s, pt: (layer, pt[b * n_pages + (n_pages - 1 - (s * g + j))], 0, 0, 0))

    kv_specs = [page_spec(j) for j in range(g)]
    n_q = n_heads * t_pad
    out = pl.pallas_call(
        functools.partial(_sb_decode_kernel, pages_per_step=g, dec_seq=dec_seq),
        grid_spec=pltpu.PrefetchScalarGridSpec(
            num_scalar_prefetch=1,
            grid=(n_seq, n_pages // g),
            in_specs=[pl.BlockSpec((n_q, 1), lambda b, s, pt: (0, 0)), head((t_pad, hd)),
                      head((hd, page)), head((hd, page))] + kv_specs + kv_specs,
            out_specs=head((t_pad, hd)),
            scratch_shapes=[pltpu.VMEM((n_heads, t_pad, hd), F32), pltpu.VMEM((n_q, 1), F32)]),
        out_shape=jax.ShapeDtypeStruct((n_seq, n_heads, t_pad, hd), F32),
        compiler_params=_params("parallel", "arbitrary"),
        name="sb_decode_attention",
    )(page_table.reshape(-1), jnp.repeat(bias, t_pad).reshape(n_q, 1), q3, new_t(qkv_s[1]),
      new_t(qkv_s[2]), *([cache_kt] * g), *([cache_vt] * g))
    return jnp.swapaxes(out[:, :, :t_new], 1, 2).reshape(n_seq, t_new, n_heads * hd)


def _rope(a, cos, sin):
    half = a.shape[-1] // 2
    a1, a2 = a[:, :half], a[:, half:]
    return jnp.concatenate([a1 * cos - a2 * sin, a1 * sin + a2 * cos], axis=-1)


def _retention_chunk(p_ref, cos, sin, s_sc, store, *, n_heads, dk, dv, valid):
    c_len = p_ref.shape[1]
    qk_w = n_heads * dk
    idx = lax.broadcasted_iota(jnp.int32, (c_len, 1), 0).astype(F32)
    rel = (lax.broadcasted_iota(jnp.int32, (c_len, c_len), 0)
           - lax.broadcasted_iota(jnp.int32, (c_len, c_len), 1)).astype(F32)
    live = lax.broadcasted_iota(jnp.int32, (c_len, 1), 0) < valid
    nt = (((1,), (1,)), ((), ()))
    tn = (((0,), (0,)), ((), ()))
    for h in range(n_heads):
        log_g = math.log1p(-2.0 ** (-5.0 - h))
        q = _rope(p_ref[0, :, h * dk:(h + 1) * dk], cos, sin)
        k = _rope(p_ref[0, :, qk_w + h * dk:qk_w + (h + 1) * dk], cos, sin) * (dk ** -0.5)
        k = jnp.where(live, k, 0.0)
        v = p_ref[1, :, h * dv:(h + 1) * dv].astype(BF16)
        gate = p_ref[2, :, h * dv:(h + 1) * dv]
        intra = jnp.where(rel >= 0.0, jnp.exp(log_g * jnp.maximum(rel, 0.0)), 0.0)
        q_dec = jnp.exp(log_g * (idx + 1.0))
        k_dec = jnp.exp(log_g * (valid - 1.0 - idx))
        qb = q.astype(BF16)
        sc = lax.dot_general(qb, k.astype(BF16), nt, preferred_element_type=F32) * intra
        s_old = s_sc[h]
        o = (jnp.dot(sc.astype(BF16), v, preferred_element_type=F32)
             + jnp.dot(qb, s_old.astype(BF16), preferred_element_type=F32) * q_dec)
        s_sc[h] = (math.exp(log_g * valid) * s_old
                   + lax.dot_general((k * k_dec).astype(BF16), v, tn, preferred_element_type=F32))
        mu = jnp.mean(o, axis=-1, keepdims=True)
        oc = o - mu
        var = jnp.mean(oc * oc, axis=-1, keepdims=True)
        on = oc * lax.rsqrt(var + LN_EPS)
        store(h, gate * (1.0 / (1.0 + jnp.exp(-gate))) * on)


def _ret_prompt_kernel(p_ref, cos_ref, sin_ref, y_in_ref, y_ref, s_out_ref, s_sc, *, n_heads, dk, dv):
    del y_in_ref
    c = pl.program_id(1)

    @pl.when(c == 0)
    def _():
        s_sc[...] = jnp.zeros_like(s_sc)

    def store(h, val):
        y_ref[:, h * dv:(h + 1) * dv] = val.astype(y_ref.dtype)

    _retention_chunk(p_ref, cos_ref[...], sin_ref[...], s_sc, store,
                     n_heads=n_heads, dk=dk, dv=dv, valid=p_ref.shape[1])

    @pl.when(c == pl.num_programs(1) - 1)
    def _():
        s_out_ref[...] = s_sc[...]


def retention_prompt(p, cos, sin, y_init, n_seq, t_len, n_heads, dk, dv):
    _, n, v_w = p.shape
    assert v_w == n_heads * dv == 2 * n_heads * dk
    chunk = _tile(t_len, RET_CHUNK, SUBLANES)
    n_chunks = t_len // chunk
    tab = pl.BlockSpec((chunk, dk // 2), lambda b, c: (c, 0))
    return pl.pallas_call(
        functools.partial(_ret_prompt_kernel, n_heads=n_heads, dk=dk, dv=dv),
        grid=(n_seq, n_chunks),
        in_specs=[pl.BlockSpec((3, chunk, v_w), lambda b, c: (0, b * n_chunks + c, 0)),
                  tab, tab, pl.BlockSpec(memory_space=pl.ANY)],
        out_specs=[pl.BlockSpec((chunk, v_w), lambda b, c: (b * n_chunks + c, 0)),
                   pl.BlockSpec((None, n_heads, dk, dv), lambda b, c: (b, 0, 0, 0))],
        out_shape=[jax.ShapeDtypeStruct((n, v_w), BF16),
                   jax.ShapeDtypeStruct((n_seq, n_heads, dk, dv), F32)],
        scratch_shapes=[pltpu.VMEM((n_heads, dk, dv), F32)],
        input_output_aliases={3: 0},
        compiler_params=_params("parallel", "arbitrary"),
        name="retention_prompt",
    )(p, cos, sin, y_init)


def _ret_decode_kernel(p_ref, cos_ref, sin_ref, s_in_ref, y_in_ref, y_ref, s_out_ref, p_sc, s_sc,
                       *, n_heads, dk, dv, dec_seq):
    del y_in_ref
    p_sc[...] = jnp.zeros_like(p_sc)
    p_sc[:, 0:SAMPLE_PAD, :] = p_ref[...]
    s_sc[...] = s_in_ref[...]

    def store(h, val):
        y_ref[:, h * dv:(h + 1) * dv] = val[0:SAMPLE_PAD].astype(y_ref.dtype)

    _retention_chunk(p_sc, cos_ref[...], sin_ref[...], s_sc, store,
                     n_heads=n_heads, dk=dk, dv=dv, valid=dec_seq)
    s_out_ref[...] = s_sc[...]


def retention_decode(p, cos, sin, state, layer, y_init, n_prompt, n_heads, dk, dv, dec_seq):
    _, n, v_w = p.shape
    n_seq = state.shape[1]
    base = n_prompt // SAMPLE_PAD
    tab = pl.BlockSpec((RET_DECODE_ROWS, dk // 2), lambda b: (0, 0))
    return pl.pallas_call(
        functools.partial(_ret_decode_kernel, n_heads=n_heads, dk=dk, dv=dv, dec_seq=dec_seq),
        grid=(n_seq,),
        in_specs=[pl.BlockSpec((3, SAMPLE_PAD, v_w), lambda b: (0, base + b, 0)), tab, tab,
                  pl.BlockSpec((None, None, n_heads, dk, dv), lambda b: (layer, b, 0, 0, 0)),
                  pl.BlockSpec(memory_space=pl.ANY)],
        out_specs=[pl.BlockSpec((SAMPLE_PAD, v_w), lambda b: (base + b, 0)),
                   pl.BlockSpec((None, n_heads, dk, dv), lambda b: (b, 0, 0, 0))],
        out_shape=[jax.ShapeDtypeStruct((n, v_w), BF16),
                   jax.ShapeDtypeStruct((n_seq, n_heads, dk, dv), F32)],
        scratch_shapes=[pltpu.VMEM((3, RET_DECODE_ROWS, v_w), F32),
                        pltpu.VMEM((n_heads, dk, dv), F32)],
        input_output_aliases={4: 0},
        compiler_params=_params("parallel"),
        name="retention_decode",
    )(p, cos, sin, state, y_init)


def _split_bf16(a):
    hi = a.astype(BF16)
    return hi, (a - hi.astype(F32)).astype(BF16)


def _router_kernel(x_ref, wt_ref, b_ref, idx_ref, gate_ref, rank_ref, cnt_ref, base_sc):
    n_exp = wt_ref.shape[0]
    tr = x_ref.shape[0]

    @pl.when(pl.program_id(0) == 0)
    def _():
        base_sc[...] = jnp.zeros_like(base_sc)

    nt = (((1,), (1,)), ((), ()))
    xh, xl = _split_bf16(x_ref[...])
    wh, wl = _split_bf16(wt_ref[...])
    logits = (lax.dot_general(wh, xh, nt, preferred_element_type=F32)
              + lax.dot_general(wh, xl, nt, preferred_element_type=F32)
              + lax.dot_general(wl, xh, nt, preferred_element_type=F32)) + b_ref[...]
    eidx = lax.broadcasted_iota(jnp.int32, (n_exp, tr), 0)
    r = lax.broadcasted_iota(jnp.int32, (tr, tr), 0)
    c = lax.broadcasted_iota(jnp.int32, (tr, tr), 1)
    before = jnp.where(r < c, 1.0, 0.0).astype(BF16)
    work = logits
    vals, sels = [], []
    seen = base_sc[...]
    for kk in range(TOP_K):
        m = jnp.max(work, axis=0, keepdims=True)
        sel = jnp.min(jnp.where(work == m, eidx, n_exp), axis=0, keepdims=True)
        hit = eidx == sel
        work = jnp.where(hit, -jnp.inf, work)
        onehot = jnp.where(hit, 1.0, 0.0)
        prefix = jnp.dot(onehot.astype(BF16), before, preferred_element_type=F32)
        rank = jnp.sum(onehot * (seen + prefix), axis=0, keepdims=True)
        rank_ref[kk:kk + 1, :] = rank.astype(jnp.int32)
        seen = seen + jnp.sum(onehot, axis=1, keepdims=True)
        vals.append(m)
        sels.append(sel)
    base_sc[...] = seen
    cnt_ref[...] = jnp.broadcast_to(seen, cnt_ref.shape)
    es = [jnp.exp(v - vals[0]) for v in vals]
    tot = es[0] + es[1] + es[2] + es[3]
    for kk in range(TOP_K):
        idx_ref[kk:kk + 1, :] = sels[kk]
        gate_ref[kk:kk + 1, :] = es[kk] / tot


def moe_router(x, w_router, b_router):
    n, d = x.shape
    n_exp = w_router.shape[1]
    tr = _tile(n, ROUTER_TILE_TARGET, LANES)
    tok = pl.BlockSpec((TOP_K, tr), lambda i: (0, i))
    idx, gates, rank, cnt = pl.pallas_call(
        _router_kernel,
        grid=(n // tr,),
        in_specs=[pl.BlockSpec((tr, d), lambda i: (i, 0)),
                  pl.BlockSpec((n_exp, d), lambda i: (0, 0)),
                  pl.BlockSpec((n_exp, 1), lambda i: (0, 0))],
        out_specs=[tok, tok, tok, pl.BlockSpec((n_exp, LANES), lambda i: (0, 0))],
        out_shape=[jax.ShapeDtypeStruct((TOP_K, n), jnp.int32),
                   jax.ShapeDtypeStruct((TOP_K, n), F32),
                   jax.ShapeDtypeStruct((TOP_K, n), jnp.int32),
                   jax.ShapeDtypeStruct((n_exp, LANES), F32)],
        scratch_shapes=[pltpu.VMEM((n_exp, 1), F32)],
        compiler_params=_params("arbitrary"),
        name="moe_router",
    )(x, w_router.T, b_router.reshape(n_exp, 1))
    return idx, gates, rank, cnt[:, 0].astype(jnp.int32)


def _deinterleave_matrix():
    r = lax.broadcasted_iota(jnp.int32, (MXU_DIM, MXU_DIM), 0)
    c = lax.broadcasted_iota(jnp.int32, (MXU_DIM, MXU_DIM), 1)
    src = jnp.where(c < MXU_DIM // 2, 2 * c, 2 * (c - MXU_DIM // 2) + 1)
    return jnp.where(r == src, 1.0, 0.0).astype(BF16)


def _expert_kernel(be_ref, nv_ref, x_ref, wu_ref, bu_ref, wd_ref, bd_ref, o_ref, wu_sc, wd_sc, glu_sc):
    i = pl.program_id(0)
    valid = i < nv_ref[0]
    fresh = (i == 0) | (be_ref[i] != be_ref[jnp.maximum(i - 1, 0)])
    n_chunks = wu_sc.shape[1] // MXU_DIM
    half = MXU_DIM // 2

    @pl.when(valid & fresh)
    def _():
        perm = _deinterleave_matrix()
        for cc in range(n_chunks):
            cols = slice(cc * MXU_DIM, (cc + 1) * MXU_DIM)
            wu_sc[:, cols] = jnp.dot(wu_ref[:, cols].astype(BF16), perm,
                                     preferred_element_type=F32).astype(BF16)
        wd_sc[...] = wd_ref[...].astype(BF16)

    @pl.when(valid)
    def _():
        xb = x_ref[...].astype(BF16)
        for cc in range(n_chunks):
            cols = slice(cc * MXU_DIM, (cc + 1) * MXU_DIM)
            h = jnp.dot(xb, wu_sc[:, cols], preferred_element_type=F32) + bu_ref[:, cols]
            gate = jnp.minimum(h[:, :half], SWIGLU_LIMIT)
            up = jnp.clip(h[:, half:], -SWIGLU_LIMIT, SWIGLU_LIMIT)
            glu = gate * (1.0 / (1.0 + jnp.exp(-SWIGLU_ALPHA * gate)))
            glu_sc[:, cc * half:(cc + 1) * half] = ((up + 1.0) * glu).astype(BF16)
        y = jnp.dot(glu_sc[...], wd_sc[...], preferred_element_type=F32) + bd_ref[...]
        o_ref[...] = y.reshape(o_ref.shape)

    @pl.when(jnp.logical_not(valid))
    def _():
        o_ref[...] = jnp.zeros_like(o_ref)


def moe_experts(xs, block_e, n_valid, w_up, b_up_perm, w_down, b_down, layer):
    n_rows, d = xs.shape
    bm = MOE_BLOCK_ROWS
    n_exp, _, two_f = w_up.shape[1:]
    f = two_f // 2
    assert two_f % MXU_DIM == 0
    return pl.pallas_call(
        _expert_kernel,
        grid_spec=pltpu.PrefetchScalarGridSpec(
            num_scalar_prefetch=2,
            grid=(n_rows // bm,),
            in_specs=[pl.BlockSpec((bm, d), lambda i, be, nv: (i, 0)),
                      pl.BlockSpec((None, None, d, two_f), lambda i, be, nv: (layer, be[i], 0, 0)),
                      pl.BlockSpec((None, None, 1, two_f), lambda i, be, nv: (layer, be[i], 0, 0)),
                      pl.BlockSpec((None, None, f, d), lambda i, be, nv: (layer, be[i], 0, 0)),
                      pl.BlockSpec((None, None, 1, d), lambda i, be, nv: (layer, be[i], 0, 0))],
            out_specs=pl.BlockSpec((bm, d // LANES, LANES), lambda i, be, nv: (i, 0, 0)),
            scratch_shapes=[pltpu.VMEM((d, two_f), BF16), pltpu.VMEM((f, d), BF16),
                            pltpu.VMEM((bm, f), BF16)]),
        out_shape=jax.ShapeDtypeStruct((n_rows, d // LANES, LANES), F32),
        compiler_params=_params("arbitrary"),
        name="moe_experts",
    )(block_e, n_valid, xs, w_up, b_up_perm, w_down, b_down)


def _deinterleave_bias(b_up):
    l, e, two_f = b_up.shape
    b = b_up.reshape(l, e, two_f // MXU_DIM, MXU_DIM // 2, 2)
    return jnp.swapaxes(b, -1, -2).reshape(l, e, 1, two_f)


def sc_gather_rows(table, idx):
    n_idx = idx.shape[0]
    row = table.shape[1:]
    n_workers = SC_CORES * SC_SUBCORES
    assert n_idx % (n_workers * SUBLANES) == 0
    per_worker = n_idx // n_workers
    chunk = _tile(per_worker, SC_GATHER_ROWS, SUBLANES)
    mesh = plsc.VectorSubcoreMesh(core_axis_name="c", subcore_axis_name="s",
                                  num_cores=SC_CORES, num_subcores=SC_SUBCORES)

    n_chunks = per_worker // chunk
    nb = SC_RING
    assert n_chunks >= nb

    @functools.partial(
        pl.kernel, mesh=mesh,
        out_type=jax.ShapeDtypeStruct((n_idx,) + row, table.dtype),
        scratch_types=([pltpu.VMEM((chunk,), jnp.int32)] * nb
                       + [pltpu.VMEM((chunk,) + row, table.dtype)] * nb
                       + [pltpu.SemaphoreType.DMA] * (2 * nb)))
    def gather(table_hbm, idx_hbm, out_hbm, *scratch):
        idxs, rows = scratch[:nb], scratch[nb:2 * nb]
        gather_sems, write_sems = scratch[2 * nb:3 * nb], scratch[3 * nb:]
        worker = lax.axis_index("s") * SC_CORES + lax.axis_index("c")
        base = worker * per_worker

        def out_rows(c):
            return out_hbm.at[pl.ds(pl.multiple_of(base + c * chunk, SUBLANES), chunk)]

        def start_gather(c, b):
            off = pl.multiple_of(base + c * chunk, SUBLANES)
            pltpu.sync_copy(idx_hbm.at[pl.ds(off, chunk)], idxs[b])
            pltpu.async_copy(table_hbm.at[idxs[b]], rows[b], gather_sems[b])

        def wait_gather(b):
            pltpu.make_async_copy(table_hbm.at[idxs[b]], rows[b], gather_sems[b]).wait()

        for b in range(nb - 1):
            start_gather(b, b)

        @pl.loop(0, n_chunks + 1, step=nb)
        def _(c):
            for b in range(nb):
                cc = c + b
                prev = (b + nb - 1) % nb

                @pl.when(cc < n_chunks)
                def _():
                    wait_gather(b)
                    pltpu.async_copy(rows[b], out_rows(cc), write_sems[b])

                @pl.when((cc >= 1) & (cc <= n_chunks))
                def _():
                    pltpu.make_async_copy(rows[prev], out_rows(cc - 1), write_sems[prev]).wait()

                @pl.when(cc + nb - 1 < n_chunks)
                def _():
                    start_gather(cc + nb - 1, prev)

    return gather(table, idx)


def _combine_ln_kernel(x_ref, y_ref, gate_ref, g_ref, b_ref, o_ref, *, alpha, n_prompt, dec_seq):
    gates = gate_ref[...]
    rows = x_ref.shape
    f = gates[:, 0:1] * y_ref[0].reshape(rows)
    for kk in range(1, TOP_K):
        f = f + gates[:, kk:kk + 1] * y_ref[kk].reshape(rows)
    y = _layer_norm(alpha * x_ref[...] + f, g_ref[...], b_ref[...])
    keep = _keep_rows(pl.program_id(0), o_ref.shape[0], n_prompt, dec_seq)
    o_ref[...] = jnp.where(keep, y, 0.0)


def combine_ln(x, y4, gates_t, g, b, alpha, n_prompt, dec_seq):
    n, d = x.shape
    tm = _tile(n, ROW_TILE_TARGET // 2, SUBLANES)
    row = pl.BlockSpec((tm, d), lambda i: (i, 0))
    vec = pl.BlockSpec((1, d), lambda i: (0, 0))
    return pl.pallas_call(
        functools.partial(_combine_ln_kernel, alpha=alpha, n_prompt=n_prompt, dec_seq=dec_seq),
        grid=(n // tm,),
        in_specs=[row, pl.BlockSpec((TOP_K, tm, d // LANES, LANES), lambda i: (0, i, 0, 0)),
                  pl.BlockSpec((tm, TOP_K), lambda i: (i, 0)), vec, vec],
        out_specs=row,
        out_shape=jax.ShapeDtypeStruct((n, d), F32),
        compiler_params=_params("parallel"),
        name="combine_ln",
    )(x, y4, gates_t, g.reshape(1, d), b.reshape(1, d))


def moe_ffn_ln(x, layer, w_router, b_router, w_up, b_up_perm, w_down, b_down, g, b,
               alpha, n_prompt, dec_seq):
    n, d = x.shape
    n_exp = w_router.shape[1]
    bm = MOE_BLOCK_ROWS
    idx, gates, rank, counts = moe_router(x, w_router, b_router)
    n_blocks = -(-(n * TOP_K + n_exp * (bm - 1)) // bm)
    padded = (counts + bm - 1) // bm * bm
    pad_end = jnp.cumsum(padded)
    pad_start = pad_end - padded
    n_valid = pad_end[-1] // bm
    blk_start = jnp.minimum(jnp.arange(n_blocks, dtype=jnp.int32), n_valid - 1) * bm
    block_e = jnp.minimum(jnp.sum(pad_end[None, :] <= blk_start[:, None], axis=1),
                          n_exp - 1).astype(jnp.int32)
    experts = jnp.arange(n_exp, dtype=jnp.int32)[:, None, None]
    dest = rank + jnp.sum(jnp.where(idx[None] == experts, pad_start[:, None, None], 0), axis=0)
    tok = jnp.broadcast_to(jnp.arange(n, dtype=jnp.int32)[None, :], (TOP_K, n))
    rows = jnp.zeros((n_blocks * bm,), jnp.int32).at[dest.reshape(-1)].set(tok.reshape(-1))
    xs = sc_gather_rows(x, rows)
    ys = moe_experts(xs, block_e, n_valid.reshape(1).astype(jnp.int32), w_up, b_up_perm,
                     w_down, b_down.reshape(b_down.shape[0], n_exp, 1, d), layer)
    y4 = sc_gather_rows(ys, dest.reshape(-1)).reshape(TOP_K, n, d // LANES, LANES)
    return combine_ln(x, y4, gates.T, g, b, alpha, n_prompt, dec_seq)


def kernel(x_prompt, x_sample, cache_k, cache_v, state_ret, page_table, w_sb_qkv, w_sb_out, sb_bias,
           w_ret_in, w_ret_out, ln_mix_g, ln_mix_b, ln_ffn_g, ln_ffn_b, w_router, b_router,
           w_exp_up, b_exp_up, w_exp_down, b_exp_down):
    n_seq, t_len, d = x_prompt.shape
    dec_batch, dec_seq, _ = x_sample.shape
    depth = ln_mix_g.shape[0]
    page, sb_heads, sb_hd = cache_k.shape[2:]
    ret_heads, dk, dv = state_ret.shape[2:]
    past_len = page_table.shape[1] * page
    alpha = (2 * depth) ** 0.25
    n_prompt = n_seq * t_len
    n_sample = dec_batch * SAMPLE_PAD
    assert dec_seq <= SAMPLE_PAD
    sb_scale = sb_hd ** -0.5
    assert math.log2(sb_scale) == round(math.log2(sb_scale))

    xs_pad = jnp.pad(x_sample, ((0, 0), (0, SAMPLE_PAD - dec_seq), (0, 0)))
    x = jnp.concatenate([x_prompt.reshape(n_prompt, d), xs_pad.reshape(n_sample, d)], axis=0)
    n = n_prompt + n_sample

    half = dk // 2
    inv = ROPE_BASE ** (-jnp.arange(half, dtype=F32) / half)
    ang_p = jnp.arange(t_len, dtype=F32)[:, None] * inv[None, :]
    ang_s = (past_len + jnp.arange(RET_DECODE_ROWS, dtype=F32))[:, None] * inv[None, :]
    b_up_perm = _deinterleave_bias(b_exp_up)
    cache_kt = jnp.transpose(cache_k, (0, 1, 3, 4, 2))
    cache_vt = jnp.transpose(cache_v, (0, 1, 3, 4, 2))

    new_k, new_v, new_s = [], [], []
    for i in range(depth):
        j = i // 2
        if i % 2 == 0:
            qkv = stacked_proj(x, w_sb_qkv[j].astype(BF16), 3, sb_scale)
            qkv_s = qkv[:, n_prompt:].reshape(3, dec_batch, SAMPLE_PAD, sb_heads, sb_hd)
            o_s = sb_decode_attention(qkv_s, sb_bias[j], cache_kt, cache_vt, page_table, j, dec_seq)
            o = jnp.zeros((n, d), BF16).at[n_prompt:].set(o_s.reshape(n_sample, d).astype(BF16))
            o = sb_prompt_attention(qkv, sb_bias[j], o, n_seq, t_len, sb_hd)
            w_out = w_sb_out[j]
            new_k.append(qkv[1])
            new_v.append(qkv[2])
        else:
            p = stacked_proj(x, w_ret_in[j].astype(BF16), 3)
            o = jnp.zeros((n, ret_heads * dv), BF16)
            o, s_s = retention_decode(p, jnp.cos(ang_s), jnp.sin(ang_s), state_ret, j, o,
                                      n_prompt, ret_heads, dk, dv, dec_seq)
            o, s_p = retention_prompt(p, jnp.cos(ang_p), jnp.sin(ang_p), o, n_seq, t_len,
                                      ret_heads, dk, dv)
            w_out = w_ret_out[j]
            new_s.append((s_p, s_s))
        x = out_proj_ln(o, w_out.astype(BF16), x, ln_mix_g[i], ln_mix_b[i], alpha, n_prompt, dec_seq)
        x = moe_ffn_ln(x, i, w_router[i], b_router[i], w_exp_up, b_up_perm, w_exp_down, b_exp_down,
                       ln_ffn_g[i], ln_ffn_b[i], alpha, n_prompt, dec_seq)

    def prompt_rows(a):
        return a[:n_prompt].reshape(n_seq, t_len, sb_heads, sb_hd)

    def sample_rows(a):
        return a[n_prompt:].reshape(dec_batch, SAMPLE_PAD, -1)[:, :dec_seq]

    y_prompt = x[:n_prompt].reshape(n_seq, t_len, d)
    y_sample = sample_rows(x)
    k_prompt = jnp.stack([prompt_rows(a) for a in new_k])
    v_prompt = jnp.stack([prompt_rows(a) for a in new_v])
    k_sample = jnp.stack([sample_rows(a).reshape(dec_batch, dec_seq, sb_heads, sb_hd) for a in new_k])
    v_sample = jnp.stack([sample_rows(a).reshape(dec_batch, dec_seq, sb_heads, sb_hd) for a in new_v])
    state_prompt = jnp.stack([s[0] for s in new_s])
    state_sample = jnp.stack([s[1] for s in new_s])
    return (y_prompt, y_sample, k_prompt, v_prompt, state_prompt, k_sample, v_sample, state_sample)
```

```python
import functools
import math

import jax
import jax.numpy as jnp
from jax import lax
from jax.experimental import pallas as pl
from jax.experimental.pallas import tpu as pltpu
from jax.experimental.pallas import tpu_sc as plsc

F32 = jnp.float32
BF16 = jnp.bfloat16

TOP_K = 4
ROPE_BASE = 10000.0
LN_EPS = 1e-5
SWIGLU_LIMIT = 7.0
SWIGLU_ALPHA = 1.702

LANES = 128
SUBLANES = 8
MXU_DIM = 256
SAMPLE_PAD = SUBLANES
VMEM_LIMIT_BYTES = 56 * 1024 * 1024
ROW_TILE_TARGET = 1024
MOE_BLOCK_ROWS = 256
SB_BLOCK = LANES
SB_QUERY_TILE = 512
SB_KEY_TILE = 512
RET_CHUNK = 256
RET_DECODE_ROWS = 128
ROUTER_TILE_TARGET = 640
DECODE_PAGES_PER_STEP = 8
DECODE_QUERY_ROWS = 16
SC_CORES = 2
SC_SUBCORES = 16
SC_GATHER_ROWS = 16
SC_RING = 4


def _params(*sem):
    return pltpu.CompilerParams(dimension_semantics=sem, vmem_limit_bytes=VMEM_LIMIT_BYTES)


def _tile(n, target, unit):
    best = None
    for t in range(unit, min(n, target) + 1, unit):
        if n % t == 0:
            best = t
    assert best is not None, (n, target, unit)
    return best


def _layer_norm(y, g, b):
    mu = jnp.mean(y, axis=-1, keepdims=True)
    yc = y - mu
    var = jnp.mean(yc * yc, axis=-1, keepdims=True)
    return yc * lax.rsqrt(var + LN_EPS) * g + b


def _keep_rows(tile_idx, tm, n_prompt, dec_seq):
    row = tile_idx * tm + lax.broadcasted_iota(jnp.int32, (tm, 1), 0)
    pad = (row >= n_prompt) & (((row - n_prompt) & (SAMPLE_PAD - 1)) >= dec_seq)
    return jnp.logical_not(pad)


def _stacked_proj_kernel(x_ref, w_ref, o_ref, *, first_scale):
    y = jnp.dot(x_ref[...].astype(BF16), w_ref[...], preferred_element_type=F32)
    if first_scale != 1.0:
        y = y * jnp.where(pl.program_id(1) == 0, first_scale, 1.0)
    o_ref[...] = y


def stacked_proj(x, w_bf16, n_groups, first_scale=1.0):
    n, k = x.shape
    width = w_bf16.shape[1] // n_groups
    tm = _tile(n, ROW_TILE_TARGET, SUBLANES)
    return pl.pallas_call(
        functools.partial(_stacked_proj_kernel, first_scale=first_scale),
        grid=(n // tm, n_groups),
        in_specs=[pl.BlockSpec((tm, k), lambda i, j: (i, 0)),
                  pl.BlockSpec((k, width), lambda i, j: (0, j))],
        out_specs=pl.BlockSpec((None, tm, width), lambda i, j: (j, i, 0)),
        out_shape=jax.ShapeDtypeStruct((n_groups, n, width), F32),
        compiler_params=_params("parallel", "arbitrary"),
        name="stacked_proj",
    )(x, w_bf16)


def _out_proj_ln_kernel(a_ref, w_ref, x_ref, g_ref, b_ref, o_ref, *, alpha, n_prompt, dec_seq):
    h = jnp.dot(a_ref[...], w_ref[...], preferred_element_type=F32)
    y = _layer_norm(alpha * x_ref[...] + h, g_ref[...], b_ref[...])
    keep = _keep_rows(pl.program_id(0), o_ref.shape[0], n_prompt, dec_seq)
    o_ref[...] = jnp.where(keep, y, 0.0)


def out_proj_ln(a, w_bf16, x, g, b, alpha, n_prompt, dec_seq):
    n, k = a.shape
    d = w_bf16.shape[1]
    tm = _tile(n, ROW_TILE_TARGET, SUBLANES)
    return pl.pallas_call(
        functools.partial(_out_proj_ln_kernel, alpha=alpha, n_prompt=n_prompt, dec_seq=dec_seq),
        grid=(n // tm,),
        in_specs=[pl.BlockSpec((tm, k), lambda i: (i, 0)),
                  pl.BlockSpec((k, d), lambda i: (0, 0)),
                  pl.BlockSpec((tm, d), lambda i: (i, 0)),
                  pl.BlockSpec((1, d), lambda i: (0, 0)),
                  pl.BlockSpec((1, d), lambda i: (0, 0))],
        out_specs=pl.BlockSpec((tm, d), lambda i: (i, 0)),
        out_shape=jax.ShapeDtypeStruct((n, d), F32),
        compiler_params=_params("parallel"),
        name="out_proj_ln",
    )(a, w_bf16, x, g.reshape(1, d), b.reshape(1, d))


def _softplus(z):
    sign = jnp.uint32(0x80000000)
    neg_abs = lax.bitcast_convert_type(lax.bitcast_convert_type(z, jnp.uint32) | sign, F32)
    return jnp.maximum(z, 0.0) + jnp.log(1.0 + jnp.exp(neg_abs))


def _suffix_sum(sp, tri):
    hi = sp.astype(BF16)
    lo = (sp - hi.astype(F32)).astype(BF16)
    return jnp.dot(jnp.concatenate([hi, lo], axis=1), tri, preferred_element_type=F32)


def _tri_incl(n):
    r = lax.broadcasted_iota(jnp.int32, (2 * n, n), 0)
    c = lax.broadcasted_iota(jnp.int32, (2 * n, n), 1)
    return jnp.where(jnp.where(r < n, r, r - n) >= c, 1.0, 0.0).astype(BF16)


def _sb_tile(q2, kbs, vbs, bias_col, carry, tri, masks):
    nt = (((1,), (1,)), ((), ()))
    zs, incls = [], []
    for kb, mask in zip(kbs, masks):
        z = lax.dot_general(q2, kb, nt, preferred_element_type=F32) + bias_col
        sp = _softplus(z)
        if mask is not None:
            sp = jnp.where(mask, sp, 0.0)
        zs.append(z)
        incls.append(_suffix_sum(sp, tri))
    pv = None
    for u in reversed(range(len(kbs))):
        a = jnp.exp(zs[u] - incls[u] - carry)
        if masks[u] is not None:
            a = jnp.where(masks[u], a, 0.0)
        part = jnp.dot(a.astype(BF16), vbs[u], preferred_element_type=F32)
        pv = part if pv is None else pv + part
        carry = carry + incls[u][:, 0:1]
    return pv, carry


def _sb_prompt_kernel(bias_ref, q_ref, k_ref, v_ref, o_in_ref, o_ref, *, head_dim, tq, tk):
    del o_in_ref
    pair = pl.program_id(1)
    n_sub = tk // SB_BLOCK
    first = lax.broadcasted_iota(jnp.int32, (tq, LANES), 1) < head_dim
    rows2 = lax.broadcasted_iota(jnp.int32, (2 * tq, 1), 0)
    bias_col = jnp.where(rows2 < tq, bias_ref[2 * pair], bias_ref[2 * pair + 1])
    tri = _tri_incl(SB_BLOCK)
    r2 = lax.broadcasted_iota(jnp.int32, (2 * tq, SB_BLOCK), 0)
    c_minus_r = lax.broadcasted_iota(jnp.int32, (2 * tq, SB_BLOCK), 1) - jnp.where(r2 < tq, r2, r2 - tq)

    def load(ref, k0, n_blocks=n_sub):
        return [ref[pl.ds(pl.multiple_of(k0 + u * SB_BLOCK, SB_BLOCK), SB_BLOCK), :].astype(BF16)
                for u in range(n_blocks)]

    def q_block(i, _):
        q0 = pl.multiple_of(i * tq, tq)
        q = q_ref[pl.ds(q0, tq), :]
        q2 = jnp.concatenate([jnp.where(first, q, 0.0), jnp.where(first, 0.0, q)], axis=0).astype(BF16)
        n_full = q0 // tk
        k0 = pl.multiple_of(n_full * tk, tk)

        def partial_tile(n_blocks):
            def run():
                masks = [c_minus_r < (q0 - k0 - u * SB_BLOCK) for u in range(n_blocks)]
                return _sb_tile(q2, load(k_ref, k0, n_blocks), load(v_ref, k0, n_blocks), bias_col,
                                jnp.zeros((2 * tq, 1), F32), tri, masks)
            return run

        if tq < tk:
            acc, carry = lax.cond(q0 == k0, partial_tile(tq // SB_BLOCK), partial_tile(n_sub))
        else:
            acc, carry = partial_tile(n_sub)()

        def k_tile(jj, state):
            acc, carry = state
            k0 = pl.multiple_of((n_full - 1 - jj) * tk, tk)
            pv, carry = _sb_tile(q2, load(k_ref, k0), load(v_ref, k0), bias_col, carry, tri,
                                 [None] * n_sub)
            return acc + pv, carry

        acc, _ = lax.fori_loop(0, n_full, k_tile, (acc, carry))
        o_ref[pl.ds(q0, tq), :] = jnp.where(first, acc[:tq], acc[tq:]).astype(o_ref.dtype)
        return 0

    lax.fori_loop(0, q_ref.shape[0] // tq, q_block, 0)


def sb_prompt_attention(qkv, bias, o_init, n_seq, t_len, head_dim):
    _, n, d = qkv.shape
    assert 2 * head_dim == LANES
    tk = min(SB_KEY_TILE, t_len)
    tq = min(SB_QUERY_TILE, tk)
    assert t_len % tk == 0 and tk % tq == 0 and tk % SB_BLOCK == 0
    part = lambda g: pl.BlockSpec((None, t_len, LANES), lambda b, p: (g, b, p))
    return pl.pallas_call(
        functools.partial(_sb_prompt_kernel, head_dim=head_dim, tq=tq, tk=tk),
        grid=(n_seq, d // LANES),
        in_specs=[pl.BlockSpec(memory_space=pltpu.SMEM), part(0), part(1), part(2),
                  pl.BlockSpec(memory_space=pl.ANY)],
        out_specs=pl.BlockSpec((t_len, LANES), lambda b, p: (b, p)),
        out_shape=jax.ShapeDtypeStruct((n, d), BF16),
        input_output_aliases={4: 0},
        compiler_params=_params("parallel", "parallel"),
        name="sb_prompt_attention",
    )(bias, qkv, qkv, qkv, o_init)


def _sb_decode_kernel(pt_ref, bias_ref, q_ref, kn_ref, vn_ref, *rest, pages_per_step, dec_seq):
    del pt_ref
    g = pages_per_step
    k_refs, v_refs = rest[:g], rest[g:2 * g]
    o_ref, acc_sc, carry_sc = rest[2 * g:]
    step = pl.program_id(1)
    n_heads, t_pad, hd = q_ref.shape
    page = k_refs[0].shape[-1]
    n_q = n_heads * t_pad
    tri = _tri_incl(page)
    q3 = q_ref[...].astype(BF16)
    bias_col = bias_ref[...]

    def page_update(kts, vts, masks):
        zs, incls = [], []
        for kt, mask in zip(kts, masks):
            z = jnp.einsum('htd,hds->hts', q3, kt, preferred_element_type=F32).reshape(n_q, page)
            z = z + bias_col
            sp = _softplus(z)
            if mask is not None:
                sp = jnp.where(mask, sp, 0.0)
            zs.append(z)
            incls.append(_suffix_sum(sp, tri))
        carry = carry_sc[...]
        acc = acc_sc[...]
        for u in reversed(range(len(kts))):
            a = jnp.exp(zs[u] - incls[u] - carry)
            if masks[u] is not None:
                a = jnp.where(masks[u], a, 0.0)
            a3 = a.reshape(n_heads, t_pad, page).astype(BF16)
            acc = acc + jnp.einsum('hts,hds->htd', a3, vts[u], preferred_element_type=F32)
            carry = carry + incls[u][:, 0:1]
        carry_sc[...] = carry
        acc_sc[...] = acc

    @pl.when(step == 0)
    def _():
        acc_sc[...] = jnp.zeros_like(acc_sc)
        carry_sc[...] = jnp.zeros_like(carry_sc)
        t_query = lax.broadcasted_iota(jnp.int32, (n_q, page), 0) & (t_pad - 1)
        s_key = lax.broadcasted_iota(jnp.int32, (n_q, page), 1)
        page_update([kn_ref[...].astype(BF16)], [vn_ref[...].astype(BF16)],
                    [(s_key < t_query) & (s_key < dec_seq)])

    page_update([kr[...].astype(BF16) for kr in reversed(k_refs)],
                [vr[...].astype(BF16) for vr in reversed(v_refs)], [None] * g)

    @pl.when(step == pl.num_programs(1) - 1)
    def _():
        o_ref[...] = acc_sc[...]


def sb_decode_attention(qkv_s, bias, cache_kt, cache_vt, page_table, layer, dec_seq):
    _, n_seq, t_new, n_heads, hd = qkv_s.shape
    page = cache_kt.shape[-1]
    n_pages = page_table.shape[1]
    g = _tile(n_pages, DECODE_PAGES_PER_STEP, 1)
    t_pad = DECODE_QUERY_ROWS
    assert t_new <= t_pad <= page and t_pad & (t_pad - 1) == 0
    q3 = jnp.pad(jnp.swapaxes(qkv_s[0], 1, 2), ((0, 0), (0, 0), (0, t_pad - t_new), (0, 0)))
    new_t = lambda a: jnp.pad(jnp.transpose(a, (0, 2, 3, 1)), ((0, 0), (0, 0), (0, 0), (0, page - t_new)))
    head = lambda last: pl.BlockSpec((None, n_heads, last[0], last[1]), lambda b, s, pt: (b, 0, 0, 0))

    def page_spec(j):
        return pl.BlockSpec(
            (None, None, n_heads, hd, page),
            lambda b, s, pt: (layer, pt[b * n_pages + (n_pages - 1 - (s * g + j))], 0, 0, 0))

    kv_specs = [page_spec(j) for j in range(g)]
    n_q = n_heads * t_pad
    out = pl.pallas_call(
        functools.partial(_sb_decode_kernel, pages_per_step=g, dec_seq=dec_seq),
        grid_spec=pltpu.PrefetchScalarGridSpec(
            num_scalar_prefetch=1,
            grid=(n_seq, n_pages // g),
            in_specs=[pl.BlockSpec((n_q, 1), lambda b, s, pt: (0, 0)), head((t_pad, hd)),
                      head((hd, page)), head((hd, page))] + kv_specs + kv_specs,
            out_specs=head((t_pad, hd)),
            scratch_shapes=[pltpu.VMEM((n_heads, t_pad, hd), F32), pltpu.VMEM((n_q, 1), F32)]),
        out_shape=jax.ShapeDtypeStruct((n_seq, n_heads, t_pad, hd), F32),
        compiler_params=_params("parallel", "arbitrary"),
        name="sb_decode_attention",
    )(page_table.reshape(-1), jnp.repeat(bias, t_pad).reshape(n_q, 1), q3, new_t(qkv_s[1]),
      new_t(qkv_s[2]), *([cache_kt] * g), *([cache_vt] * g))
    return jnp.swapaxes(out[:, :, :t_new], 1, 2).reshape(n_seq, t_new, n_heads * hd)


def _rope(a, cos, sin):
    half = a.shape[-1] // 2
    a1, a2 = a[:, :half], a[:, half:]
    return jnp.concatenate([a1 * cos - a2 * sin, a1 * sin + a2 * cos], axis=-1)


def _retention_chunk(p_ref, cos, sin, s_sc, store, *, n_heads, dk, dv, valid):
    c_len = p_ref.shape[1]
    qk_w = n_heads * dk
    idx = lax.broadcasted_iota(jnp.int32, (c_len, 1), 0).astype(F32)
    rel = (lax.broadcasted_iota(jnp.int32, (c_len, c_len), 0)
           - lax.broadcasted_iota(jnp.int32, (c_len, c_len), 1)).astype(F32)
    live = lax.broadcasted_iota(jnp.int32, (c_len, 1), 0) < valid
    nt = (((1,), (1,)), ((), ()))
    tn = (((0,), (0,)), ((), ()))
    for h in range(n_heads):
        log_g = math.log1p(-2.0 ** (-5.0 - h))
        q = _rope(p_ref[0, :, h * dk:(h + 1) * dk], cos, sin)
        k = _rope(p_ref[0, :, qk_w + h * dk:qk_w + (h + 1) * dk], cos, sin) * (dk ** -0.5)
        k = jnp.where(live, k, 0.0)
        v = p_ref[1, :, h * dv:(h + 1) * dv].astype(BF16)
        gate = p_ref[2, :, h * dv:(h + 1) * dv]
        intra = jnp.where(rel >= 0.0, jnp.exp(log_g * jnp.maximum(rel, 0.0)), 0.0)
        q_dec = jnp.exp(log_g * (idx + 1.0))
        k_dec = jnp.exp(log_g * (valid - 1.0 - idx))
        qb = q.astype(BF16)
        sc = lax.dot_general(qb, k.astype(BF16), nt, preferred_element_type=F32) * intra
        s_old = s_sc[h]
        o = (jnp.dot(sc.astype(BF16), v, preferred_element_type=F32)
             + jnp.dot(qb, s_old.astype(BF16), preferred_element_type=F32) * q_dec)
        s_sc[h] = (math.exp(log_g * valid) * s_old
                   + lax.dot_general((k * k_dec).astype(BF16), v, tn, preferred_element_type=F32))
        mu = jnp.mean(o, axis=-1, keepdims=True)
        oc = o - mu
        var = jnp.mean(oc * oc, axis=-1, keepdims=True)
        on = oc * lax.rsqrt(var + LN_EPS)
        store(h, gate * (1.0 / (1.0 + jnp.exp(-gate))) * on)


def _ret_prompt_kernel(p_ref, cos_ref, sin_ref, y_in_ref, y_ref, s_out_ref, s_sc, *, n_heads, dk, dv):
    del y_in_ref
    c = pl.program_id(1)

    @pl.when(c == 0)
    def _():
        s_sc[...] = jnp.zeros_like(s_sc)

    def store(h, val):
        y_ref[:, h * dv:(h + 1) * dv] = val.astype(y_ref.dtype)

    _retention_chunk(p_ref, cos_ref[...], sin_ref[...], s_sc, store,
                     n_heads=n_heads, dk=dk, dv=dv, valid=p_ref.shape[1])

    @pl.when(c == pl.num_programs(1) - 1)
    def _():
        s_out_ref[...] = s_sc[...]


def retention_prompt(p, cos, sin, y_init, n_seq, t_len, n_heads, dk, dv):
    _, n, v_w = p.shape
    assert v_w == n_heads * dv == 2 * n_heads * dk
    chunk = _tile(t_len, RET_CHUNK, SUBLANES)
    n_chunks = t_len // chunk
    tab = pl.BlockSpec((chunk, dk // 2), lambda b, c: (c, 0))
    return pl.pallas_call(
        functools.partial(_ret_prompt_kernel, n_heads=n_heads, dk=dk, dv=dv),
        grid=(n_seq, n_chunks),
        in_specs=[pl.BlockSpec((3, chunk, v_w), lambda b, c: (0, b * n_chunks + c, 0)),
                  tab, tab, pl.BlockSpec(memory_space=pl.ANY)],
        out_specs=[pl.BlockSpec((chunk, v_w), lambda b, c: (b * n_chunks + c, 0)),
                   pl.BlockSpec((None, n_heads, dk, dv), lambda b, c: (b, 0, 0, 0))],
        out_shape=[jax.ShapeDtypeStruct((n, v_w), BF16),
                   jax.ShapeDtypeStruct((n_seq, n_heads, dk, dv), F32)],
        scratch_shapes=[pltpu.VMEM((n_heads, dk, dv), F32)],
        input_output_aliases={3: 0},
        compiler_params=_params("parallel", "arbitrary"),
        name="retention_prompt",
    )(p, cos, sin, y_init)


def _ret_decode_kernel(p_ref, cos_ref, sin_ref, s_in_ref, y_in_ref, y_ref, s_out_ref, p_sc, s_sc,
                       *, n_heads, dk, dv, dec_seq):
    del y_in_ref
    p_sc[...] = jnp.zeros_like(p_sc)
    p_sc[:, 0:SAMPLE_PAD, :] = p_ref[...]
    s_sc[...] = s_in_ref[...]

    def store(h, val):
        y_ref[:, h * dv:(h + 1) * dv] = val[0:SAMPLE_PAD].astype(y_ref.dtype)

    _retention_chunk(p_sc, cos_ref[...], sin_ref[...], s_sc, store,
                     n_heads=n_heads, dk=dk, dv=dv, valid=dec_seq)
    s_out_ref[...] = s_sc[...]


def retention_decode(p, cos, sin, state, layer, y_init, n_prompt, n_heads, dk, dv, dec_seq):
    _, n, v_w = p.shape
    n_seq = state.shape[1]
    base = n_prompt // SAMPLE_PAD
    tab = pl.BlockSpec((RET_DECODE_ROWS, dk // 2), lambda b: (0, 0))
    return pl.pallas_call(
        functools.partial(_ret_decode_kernel, n_heads=n_heads, dk=dk, dv=dv, dec_seq=dec_seq),
        grid=(n_seq,),
        in_specs=[pl.BlockSpec((3, SAMPLE_PAD, v_w), lambda b: (0, base + b, 0)), tab, tab,
                  pl.BlockSpec((None, None, n_heads, dk, dv), lambda b: (layer, b, 0, 0, 0)),
                  pl.BlockSpec(memory_space=pl.ANY)],
        out_specs=[pl.BlockSpec((SAMPLE_PAD, v_w), lambda b: (base + b, 0)),
                   pl.BlockSpec((None, n_heads, dk, dv), lambda b: (b, 0, 0, 0))],
        out_shape=[jax.ShapeDtypeStruct((n, v_w), BF16),
                   jax.ShapeDtypeStruct((n_seq, n_heads, dk, dv), F32)],
        scratch_shapes=[pltpu.VMEM((3, RET_DECODE_ROWS, v_w), F32),
                        pltpu.VMEM((n_heads, dk, dv), F32)],
        input_output_aliases={4: 0},
        compiler_params=_params("parallel"),
        name="retention_decode",
    )(p, cos, sin, state, y_init)


def _split_bf16(a):
    hi = a.astype(BF16)
    return hi, (a - hi.astype(F32)).astype(BF16)


def _router_kernel(x_ref, wt_ref, b_ref, idx_ref, gate_ref, rank_ref, cnt_ref, base_sc):
    n_exp = wt_ref.shape[0]
    tr = x_ref.shape[0]

    @pl.when(pl.program_id(0) == 0)
    def _():
        base_sc[...] = jnp.zeros_like(base_sc)

    nt = (((1,), (1,)), ((), ()))
    xh, xl = _split_bf16(x_ref[...])
    wh, wl = _split_bf16(wt_ref[...])
    logits = (lax.dot_general(wh, xh, nt, preferred_element_type=F32)
              + lax.dot_general(wh, xl, nt, preferred_element_type=F32)
              + lax.dot_general(wl, xh, nt, preferred_element_type=F32)) + b_ref[...]
    eidx = lax.broadcasted_iota(jnp.int32, (n_exp, tr), 0)
    r = lax.broadcasted_iota(jnp.int32, (tr, tr), 0)
    c = lax.broadcasted_iota(jnp.int32, (tr, tr), 1)
    before = jnp.where(r < c, 1.0, 0.0).astype(BF16)
    work = logits
    vals, sels = [], []
    seen = base_sc[...]
    for kk in range(TOP_K):
        m = jnp.max(work, axis=0, keepdims=True)
        sel = jnp.min(jnp.where(work == m, eidx, n_exp), axis=0, keepdims=True)
        hit = eidx == sel
        work = jnp.where(hit, -jnp.inf, work)
        onehot = jnp.where(hit, 1.0, 0.0)
        prefix = jnp.dot(onehot.astype(BF16), before, preferred_element_type=F32)
        rank = jnp.sum(onehot * (seen + prefix), axis=0, keepdims=True)
        rank_ref[kk:kk + 1, :] = rank.astype(jnp.int32)
        seen = seen + jnp.sum(onehot, axis=1, keepdims=True)
        vals.append(m)
        sels.append(sel)
    base_sc[...] = seen
    cnt_ref[...] = jnp.broadcast_to(seen, cnt_ref.shape)
    es = [jnp.exp(v - vals[0]) for v in vals]
    tot = es[0] + es[1] + es[2] + es[3]
    for kk in range(TOP_K):
        idx_ref[kk:kk + 1, :] = sels[kk]
        gate_ref[kk:kk + 1, :] = es[kk] / tot


def moe_router(x, w_router, b_router):
    n, d = x.shape
    n_exp = w_router.shape[1]
    tr = _tile(n, ROUTER_TILE_TARGET, LANES)
    tok = pl.BlockSpec((TOP_K, tr), lambda i: (0, i))
    idx, gates, rank, cnt = pl.pallas_call(
        _router_kernel,
        grid=(n // tr,),
        in_specs=[pl.BlockSpec((tr, d), lambda i: (i, 0)),
                  pl.BlockSpec((n_exp, d), lambda i: (0, 0)),
                  pl.BlockSpec((n_exp, 1), lambda i: (0, 0))],
        out_specs=[tok, tok, tok, pl.BlockSpec((n_exp, LANES), lambda i: (0, 0))],
        out_shape=[jax.ShapeDtypeStruct((TOP_K, n), jnp.int32),
                   jax.ShapeDtypeStruct((TOP_K, n), F32),
                   jax.ShapeDtypeStruct((TOP_K, n), jnp.int32),
                   jax.ShapeDtypeStruct((n_exp, LANES), F32)],
        scratch_shapes=[pltpu.VMEM((n_exp, 1), F32)],
        compiler_params=_params("arbitrary"),
        name="moe_router",
    )(x, w_router.T, b_router.reshape(n_exp, 1))
    return idx, gates, rank, cnt[:, 0].astype(jnp.int32)


def _deinterleave_matrix():
    r = lax.broadcasted_iota(jnp.int32, (MXU_DIM, MXU_DIM), 0)
    c = lax.broadcasted_iota(jnp.int32, (MXU_DIM, MXU_DIM), 1)
    src = jnp.where(c < MXU_DIM // 2, 2 * c, 2 * (c - MXU_DIM // 2) + 1)
    return jnp.where(r == src, 1.0, 0.0).astype(BF16)


def _expert_kernel(be_ref, nv_ref, x_ref, wu_ref, bu_ref, wd_ref, bd_ref, o_ref, wu_sc, wd_sc, glu_sc):
    i = pl.program_id(0)
    valid = i < nv_ref[0]
    fresh = (i == 0) | (be_ref[i] != be_ref[jnp.maximum(i - 1, 0)])
    n_chunks = wu_sc.shape[1] // MXU_DIM
    half = MXU_DIM // 2

    @pl.when(valid & fresh)
    def _():
        perm = _deinterleave_matrix()
        for cc in range(n_chunks):
            cols = slice(cc * MXU_DIM, (cc + 1) * MXU_DIM)
            wu_sc[:, cols] = jnp.dot(wu_ref[:, cols].astype(BF16), perm,
                                     preferred_element_type=F32).astype(BF16)
        wd_sc[...] = wd_ref[...].astype(BF16)

    @pl.when(valid)
    def _():
        xb = x_ref[...].astype(BF16)
        for cc in range(n_chunks):
            cols = slice(cc * MXU_DIM, (cc + 1) * MXU_DIM)
            h = jnp.dot(xb, wu_sc[:, cols], preferred_element_type=F32) + bu_ref[:, cols]
            gate = jnp.minimum(h[:, :half], SWIGLU_LIMIT)
            up = jnp.clip(h[:, half:], -SWIGLU_LIMIT, SWIGLU_LIMIT)
            glu = gate * (1.0 / (1.0 + jnp.exp(-SWIGLU_ALPHA * gate)))
            glu_sc[:, cc * half:(cc + 1) * half] = ((up + 1.0) * glu).astype(BF16)
        y = jnp.dot(glu_sc[...], wd_sc[...], preferred_element_type=F32) + bd_ref[...]
        o_ref[...] = y.reshape(o_ref.shape)

    @pl.when(jnp.logical_not(valid))
    def _():
        o_ref[...] = jnp.zeros_like(o_ref)


def moe_experts(xs, block_e, n_valid, w_up, b_up_perm, w_down, b_down, layer):
    n_rows, d = xs.shape
    bm = MOE_BLOCK_ROWS
    n_exp, _, two_f = w_up.shape[1:]
    f = two_f // 2
    assert two_f % MXU_DIM == 0
    return pl.pallas_call(
        _expert_kernel,
        grid_spec=pltpu.PrefetchScalarGridSpec(
            num_scalar_prefetch=2,
            grid=(n_rows // bm,),
            in_specs=[pl.BlockSpec((bm, d), lambda i, be, nv: (i, 0)),
                      pl.BlockSpec((None, None, d, two_f), lambda i, be, nv: (layer, be[i], 0, 0)),
                      pl.BlockSpec((None, None, 1, two_f), lambda i, be, nv: (layer, be[i], 0, 0)),
                      pl.BlockSpec((None, None, f, d), lambda i, be, nv: (layer, be[i], 0, 0)),
                      pl.BlockSpec((None, None, 1, d), lambda i, be, nv: (layer, be[i], 0, 0))],
            out_specs=pl.BlockSpec((bm, d // LANES, LANES), lambda i, be, nv: (i, 0, 0)),
            scratch_shapes=[pltpu.VMEM((d, two_f), BF16), pltpu.VMEM((f, d), BF16),
                            pltpu.VMEM((bm, f), BF16)]),
        out_shape=jax.ShapeDtypeStruct((n_rows, d // LANES, LANES), F32),
        compiler_params=_params("arbitrary"),
        name="moe_experts",
    )(block_e, n_valid, xs, w_up, b_up_perm, w_down, b_down)


def _deinterleave_bias(b_up):
    l, e, two_f = b_up.shape
    b = b_up.reshape(l, e, two_f // MXU_DIM, MXU_DIM // 2, 2)
    return jnp.swapaxes(b, -1, -2).reshape(l, e, 1, two_f)


def sc_gather_rows(table, idx):
    n_idx = idx.shape[0]
    row = table.shape[1:]
    n_workers = SC_CORES * SC_SUBCORES
    assert n_idx % (n_workers * SUBLANES) == 0
    per_worker = n_idx // n_workers
    chunk = _tile(per_worker, SC_GATHER_ROWS, SUBLANES)
    mesh = plsc.VectorSubcoreMesh(core_axis_name="c", subcore_axis_name="s",
                                  num_cores=SC_CORES, num_subcores=SC_SUBCORES)

    n_chunks = per_worker // chunk
    nb = SC_RING
    assert n_chunks >= nb

    @functools.partial(
        pl.kernel, mesh=mesh,
        out_type=jax.ShapeDtypeStruct((n_idx,) + row, table.dtype),
        scratch_types=([pltpu.VMEM((chunk,), jnp.int32)] * nb
                       + [pltpu.VMEM((chunk,) + row, table.dtype)] * nb
                       + [pltpu.SemaphoreType.DMA] * (2 * nb)))
    def gather(table_hbm, idx_hbm, out_hbm, *scratch):
        idxs, rows = scratch[:nb], scratch[nb:2 * nb]
        gather_sems, write_sems = scratch[2 * nb:3 * nb], scratch[3 * nb:]
        worker = lax.axis_index("s") * SC_CORES + lax.axis_index("c")
        base = worker * per_worker

        def out_rows(c):
            return out_hbm.at[pl.ds(pl.multiple_of(base + c * chunk, SUBLANES), chunk)]

        def start_gather(c, b):
            off = pl.multiple_of(base + c * chunk, SUBLANES)
            pltpu.sync_copy(idx_hbm.at[pl.ds(off, chunk)], idxs[b])
            pltpu.async_copy(table_hbm.at[idxs[b]], rows[b], gather_sems[b])

        def wait_gather(b):
            pltpu.make_async_copy(table_hbm.at[idxs[b]], rows[b], gather_sems[b]).wait()

        for b in range(nb - 1):
            start_gather(b, b)

        @pl.loop(0, n_chunks + 1, step=nb)
        def _(c):
            for b in range(nb):
                cc = c + b
                prev = (b + nb - 1) % nb

                @pl.when(cc < n_chunks)
                def _():
                    wait_gather(b)
                    pltpu.async_copy(rows[b], out_rows(cc), write_sems[b])

                @pl.when((cc >= 1) & (cc <= n_chunks))
                def _():
                    pltpu.make_async_copy(rows[prev], out_rows(cc - 1), write_sems[prev]).wait()

                @pl.when(cc + nb - 1 < n_chunks)
                def _():
                    start_gather(cc + nb - 1, prev)

    return gather(table, idx)


def sc_scatter_rows(src, idx, n_out):
    n_src = src.shape[0]
    row = src.shape[1:]
    n_workers = SC_CORES * SC_SUBCORES
    assert n_src % (n_workers * SUBLANES) == 0
    per_worker = n_src // n_workers
    chunk = _tile(per_worker, SC_GATHER_ROWS, SUBLANES)
    mesh = plsc.VectorSubcoreMesh(core_axis_name="c", subcore_axis_name="s",
                                  num_cores=SC_CORES, num_subcores=SC_SUBCORES)
    n_chunks = per_worker // chunk
    nb = SC_RING
    assert n_chunks >= nb

    @functools.partial(
        pl.kernel, mesh=mesh,
        out_type=jax.ShapeDtypeStruct((n_out,) + row, src.dtype),
        scratch_types=([pltpu.VMEM((chunk,), jnp.int32)] * nb
                       + [pltpu.VMEM((chunk,) + row, src.dtype)] * nb
                       + [pltpu.SemaphoreType.DMA] * (2 * nb)))
    def scatter(src_hbm, idx_hbm, out_hbm, *scratch):
        idxs, rows = scratch[:nb], scratch[nb:2 * nb]
        load_sems, write_sems = scratch[2 * nb:3 * nb], scratch[3 * nb:]
        worker = lax.axis_index("s") * SC_CORES + lax.axis_index("c")
        base = worker * per_worker

        def src_rows(c):
            return src_hbm.at[pl.ds(pl.multiple_of(base + c * chunk, SUBLANES), chunk)]

        def start_load(c, b):
            off = pl.multiple_of(base + c * chunk, SUBLANES)
            pltpu.sync_copy(idx_hbm.at[pl.ds(off, chunk)], idxs[b])
            pltpu.async_copy(src_rows(c), rows[b], load_sems[b])

        for b in range(nb - 1):
            start_load(b, b)

        @pl.loop(0, n_chunks + 1, step=nb)
        def _(c):
            for b in range(nb):
                cc = c + b
                prev = (b + nb - 1) % nb

                @pl.when(cc < n_chunks)
                def _():
                    pltpu.make_async_copy(src_rows(cc), rows[b], load_sems[b]).wait()
                    pltpu.async_copy(rows[b], out_hbm.at[idxs[b]], write_sems[b])

                @pl.when((cc >= 1) & (cc <= n_chunks))
                def _():
                    pltpu.make_async_copy(rows[prev], out_hbm.at[idxs[prev]], write_sems[prev]).wait()

                @pl.when(cc + nb - 1 < n_chunks)
                def _():
                    start_load(cc + nb - 1, prev)

    return scatter(src, idx)


def _combine_ln_kernel(x_ref, y0_ref, y1_ref, y2_ref, y3_ref, gate_ref, g_ref, b_ref, o_ref,
                       *, alpha, n_prompt, dec_seq):
    gates = gate_ref[...]
    rows = x_ref.shape
    y_refs = (y0_ref, y1_ref, y2_ref, y3_ref)
    f = gates[:, 0:1] * y_refs[0][...].reshape(rows)
    for kk in range(1, TOP_K):
        f = f + gates[:, kk:kk + 1] * y_refs[kk][...].reshape(rows)
    y = _layer_norm(alpha * x_ref[...] + f, g_ref[...], b_ref[...])
    keep = _keep_rows(pl.program_id(0), o_ref.shape[0], n_prompt, dec_seq)
    o_ref[...] = jnp.where(keep, y, 0.0)


def combine_ln(x, y_slots, gates_t, g, b, alpha, n_prompt, dec_seq):
    n, d = x.shape
    assert TOP_K == 4
    tm = _tile(n, ROW_TILE_TARGET // 2, SUBLANES)
    row = pl.BlockSpec((tm, d), lambda i: (i, 0))
    vec = pl.BlockSpec((1, d), lambda i: (0, 0))
    slot = lambda kk: pl.BlockSpec((tm, d // LANES, LANES), lambda i: (kk * (n // tm) + i, 0, 0))
    return pl.pallas_call(
        functools.partial(_combine_ln_kernel, alpha=alpha, n_prompt=n_prompt, dec_seq=dec_seq),
        grid=(n // tm,),
        in_specs=[row, slot(0), slot(1), slot(2), slot(3),
                  pl.BlockSpec((tm, TOP_K), lambda i: (i, 0)), vec, vec],
        out_specs=row,
        out_shape=jax.ShapeDtypeStruct((n, d), F32),
        compiler_params=_params("parallel"),
        name="combine_ln",
    )(x, y_slots, y_slots, y_slots, y_slots, gates_t, g.reshape(1, d), b.reshape(1, d))


def moe_ffn_ln(x, layer, w_router, b_router, w_up, b_up_perm, w_down, b_down, g, b,
               alpha, n_prompt, dec_seq):
    n, d = x.shape
    n_exp = w_router.shape[1]
    bm = MOE_BLOCK_ROWS
    idx, gates, rank, counts = moe_router(x, w_router, b_router)
    n_blocks = -(-(n * TOP_K + n_exp * (bm - 1)) // bm)
    padded = (counts + bm - 1) // bm * bm
    pad_end = jnp.cumsum(padded)
    pad_start = pad_end - padded
    n_valid = pad_end[-1] // bm
    blk_start = jnp.minimum(jnp.arange(n_blocks, dtype=jnp.int32), n_valid - 1) * bm
    block_e = jnp.minimum(jnp.sum(pad_end[None, :] <= blk_start[:, None], axis=1),
                          n_exp - 1).astype(jnp.int32)
    experts = jnp.arange(n_exp, dtype=jnp.int32)[:, None, None]
    dest = rank + jnp.sum(jnp.where(idx[None] == experts, pad_start[:, None, None], 0), axis=0)
    n_slots = TOP_K * n
    slot_of_row = jnp.full((n_blocks * bm,), n_slots, jnp.int32).at[dest.reshape(-1)].set(
        jnp.arange(n_slots, dtype=jnp.int32))
    rows = jnp.where(slot_of_row < n_slots, slot_of_row % n, 0)
    xs = sc_gather_rows(x, rows)
    ys = moe_experts(xs, block_e, n_valid.reshape(1).astype(jnp.int32), w_up, b_up_perm,
                     w_down, b_down.reshape(b_down.shape[0], n_exp, 1, d), layer)
    y_slots = sc_scatter_rows(ys, slot_of_row, n_slots + SUBLANES)
    return combine_ln(x, y_slots, gates.T, g, b, alpha, n_prompt, dec_seq)


def kernel(x_prompt, x_sample, cache_k, cache_v, state_ret, page_table, w_sb_qkv, w_sb_out, sb_bias,
           w_ret_in, w_ret_out, ln_mix_g, ln_mix_b, ln_ffn_g, ln_ffn_b, w_router, b_router,
           w_exp_up, b_exp_up, w_exp_down, b_exp_down):
    n_seq, t_len, d = x_prompt.shape
    dec_batch, dec_seq, _ = x_sample.shape
    depth = ln_mix_g.shape[0]
    page, sb_heads, sb_hd = cache_k.shape[2:]
    ret_heads, dk, dv = state_ret.shape[2:]
    past_len = page_table.shape[1] * page
    alpha = (2 * depth) ** 0.25
    n_prompt = n_seq * t_len
    n_sample = dec_batch * SAMPLE_PAD
    assert dec_seq <= SAMPLE_PAD
    sb_scale = sb_hd ** -0.5
    assert math.log2(sb_scale) == round(math.log2(sb_scale))

    xs_pad = jnp.pad(x_sample, ((0, 0), (0, SAMPLE_PAD - dec_seq), (0, 0)))
    x = jnp.concatenate([x_prompt.reshape(n_prompt, d), xs_pad.reshape(n_sample, d)], axis=0)
    n = n_prompt + n_sample

    half = dk // 2
    inv = ROPE_BASE ** (-jnp.arange(half, dtype=F32) / half)
    ang_p = jnp.arange(t_len, dtype=F32)[:, None] * inv[None, :]
    ang_s = (past_len + jnp.arange(RET_DECODE_ROWS, dtype=F32))[:, None] * inv[None, :]
    b_up_perm = _deinterleave_bias(b_exp_up)
    cache_kt = jnp.transpose(cache_k, (0, 1, 3, 4, 2))
    cache_vt = jnp.transpose(cache_v, (0, 1, 3, 4, 2))

    new_k, new_v, new_s = [], [], []
    for i in range(depth):
        j = i // 2
        if i % 2 == 0:
            qkv = stacked_proj(x, w_sb_qkv[j].astype(BF16), 3, sb_scale)
            qkv_s = qkv[:, n_prompt:].reshape(3, dec_batch, SAMPLE_PAD, sb_heads, sb_hd)
            o_s = sb_decode_attention(qkv_s, sb_bias[j], cache_kt, cache_vt, page_table, j, dec_seq)
            o = jnp.zeros((n, d), BF16).at[n_prompt:].set(o_s.reshape(n_sample, d).astype(BF16))
            o = sb_prompt_attention(qkv, sb_bias[j], o, n_seq, t_len, sb_hd)
            w_out = w_sb_out[j]
            new_k.append(qkv[1])
            new_v.append(qkv[2])
        else:
            p = stacked_proj(x, w_ret_in[j].astype(BF16), 3)
            o = jnp.zeros((n, ret_heads * dv), BF16)
            o, s_s = retention_decode(p, jnp.cos(ang_s), jnp.sin(ang_s), state_ret, j, o,
                                      n_prompt, ret_heads, dk, dv, dec_seq)
            o, s_p = retention_prompt(p, jnp.cos(ang_p), jnp.sin(ang_p), o, n_seq, t_len,
                                      ret_heads, dk, dv)
            w_out = w_ret_out[j]
            new_s.append((s_p, s_s))
        x = out_proj_ln(o, w_out.astype(BF16), x, ln_mix_g[i], ln_mix_b[i], alpha, n_prompt, dec_seq)
        x = moe_ffn_ln(x, i, w_router[i], b_router[i], w_exp_up, b_up_perm, w_exp_down, b_exp_down,
                       ln_ffn_g[i], ln_ffn_b[i], alpha, n_prompt, dec_seq)

    def prompt_rows(a):
        return a[:n_prompt].reshape(n_seq, t_len, sb_heads, sb_hd)

    def sample_rows(a):
        return a[n_prompt:].reshape(dec_batch, SAMPLE_PAD, -1)[:, :dec_seq]

    y_prompt = x[:n_prompt].reshape(n_seq, t_len, d)
    y_sample = sample_rows(x)
    k_prompt = jnp.stack([prompt_rows(a) for a in new_k])
    v_prompt = jnp.stack([prompt_rows(a) for a in new_v])
    k_sample = jnp.stack([sample_rows(a).reshape(dec_batch, dec_seq, sb_heads, sb_hd) for a in new_k])
    v_sample = jnp.stack([sample_rows(a).reshape(dec_batch, dec_seq, sb_heads, sb_hd) for a in new_v])
    state_prompt = jnp.stack([s[0] for s in new_s])
    state_sample = jnp.stack([s[1] for s in new_s])
    return (y_prompt, y_sample, k_prompt, v_prompt, state_prompt, k_sample, v_sample, state_sample)
```
